```python
import math
import functools
import jax
import jax.numpy as jnp
from jax import lax
import numpy as np

D_MODEL = 1024
BATCH = 2
SEQ = 8192
DEPTH = 1
DEC_BATCH = 128
DEC_SEQ = 4
PAST_LEN = 2048
PAGE_SIZE = 128

DA_DK = 64
DA_DV = 2 * DA_DK
DA_HEADS = D_MODEL // DA_DV
DA_QK_W = 2 * DA_HEADS * DA_DK
DA_V_W = DA_HEADS * DA_DV
RET_HEADS = 4
RET_DK = D_MODEL // RET_HEADS
RET_DV = D_MODEL // RET_HEADS
RET_QK_W = RET_HEADS * RET_DK
RET_V_W = RET_HEADS * RET_DV
RET_CHUNK = 128
RET_THETA = 10000.0
IN_WIDTHS = (DA_QK_W, DA_QK_W, DA_V_W, RET_QK_W, RET_QK_W, RET_V_W, RET_V_W, D_MODEL, D_MODEL)
W_IN_COLS = sum(IN_WIDTHS)
Q_BLOCK = 128
D_FF = 2816
HALF_STEP = 0.5
ROPE_THETA = 10000.0
NORM_EPS = 1e-6
SUBLN_EPS = 1e-5
GN_EPS = 1e-5
N_MOD = 9
NEG_INF = -1e30

kernel_name = 'hybrid_diffattn_retention_macaron_step'


def rms_norm(x, g, eps=NORM_EPS):
    xf = x.astype(jnp.float32)
    y = xf * lax.rsqrt(jnp.mean(xf * xf, axis=-1, keepdims=True) + eps)
    return (y * g.astype(jnp.float32)).astype(x.dtype)


def group_norm_heads(x, g, eps=GN_EPS):
    xf = x.astype(jnp.float32)
    mu = jnp.mean(xf, axis=-1, keepdims=True)
    var = jnp.mean(jnp.square(xf - mu), axis=-1, keepdims=True)
    y = (xf - mu) * lax.rsqrt(var + eps)
    return y * g.reshape(x.shape[-2], x.shape[-1]).astype(jnp.float32)


def modulate(x, g, shift, scale):
    return rms_norm(x, g) * (1 + scale[:, None, :]) + shift[:, None, :]


def swiglu_ffn(h, w_in, w_out):
    a, b = jnp.split(h @ w_in, 2, axis=-1)
    return (jax.nn.silu(a) * b) @ w_out


def rope_half(x, pos):
    half = x.shape[-1] // 2
    inv = ROPE_THETA ** (-jnp.arange(half, dtype=jnp.float32) / half)
    ang = pos.astype(jnp.float32)[:, None] * inv[None, :]
    cos = jnp.cos(ang)[None, :, None, :]
    sin = jnp.sin(ang)[None, :, None, :]
    xf = x.astype(jnp.float32)
    x1, x2 = xf[..., :half], xf[..., half:]
    return jnp.concatenate([x1 * cos - x2 * sin, x2 * cos + x1 * sin], axis=-1).astype(x.dtype)


def retnet_rotate(x, pos):
    half = x.shape[-1] // 2
    angle = 1.0 / (RET_THETA ** jnp.linspace(0.0, 1.0, half, dtype=jnp.float32))
    ang = pos.astype(jnp.float32)[:, None] * angle[None, :]
    cos = jnp.cos(ang)[None, :, None, :]
    sin = jnp.sin(ang)[None, :, None, :]
    xf = x.astype(jnp.float32).reshape(*x.shape[:-1], half, 2)
    x0, x1 = xf[..., 0], xf[..., 1]
    out = jnp.stack([x0 * cos - x1 * sin, x1 * cos + x0 * sin], axis=-1).reshape(x.shape)
    return out.astype(x.dtype)


def retention_log_decay():
    return jnp.log1p(-jnp.exp2(-5.0 - jnp.arange(RET_HEADS, dtype=jnp.float32)))


def retention_chunk(q, k, v, state, log_g):
    L = q.shape[1]
    qf, kf, vf = q.astype(jnp.float32), k.astype(jnp.float32), v.astype(jnp.float32)
    idx = jnp.arange(L, dtype=jnp.float32)
    dist = idx[:, None] - idx[None, :]
    decay = jnp.where(dist[None] >= 0, jnp.exp(jnp.maximum(dist, 0.0)[None] * log_g[:, None, None]), 0.0)
    scores = jnp.einsum('blhd,bshd->bhls', qf, kf) * decay[None]
    inner = jnp.einsum('bhls,bshv->blhv', scores, vf)
    q_decay = jnp.exp((idx + 1.0)[:, None] * log_g[None, :])
    cross = jnp.einsum('blhd,bhdv->blhv', qf, state) * q_decay[None, :, :, None]
    k_decay = jnp.exp((L - 1.0 - idx)[:, None] * log_g[None, :])
    new_state = (jnp.exp(L * log_g)[None, :, None, None] * state
                 + jnp.einsum('bshd,bshv->bhdv', kf * k_decay[None, :, :, None], vf))
    return inner + cross, new_state


def retention_prompt(q, k, v, log_g):
    B, T = q.shape[:2]
    n_chunks = T // RET_CHUNK

    def chunks(a):
        return a.reshape(B, n_chunks, RET_CHUNK, *a.shape[2:]).swapaxes(0, 1)

    state0 = jnp.zeros((B, RET_HEADS, RET_DK, RET_DV), jnp.float32)

    def step(state, qkv):
        o, state = retention_chunk(*qkv, state, log_g)
        return state, o

    state, o = lax.scan(step, state0, (chunks(q), chunks(k), chunks(v)))
    return o.swapaxes(0, 1).reshape(B, T, RET_HEADS, RET_DV), state


def diff_attend(q, k, v, q_pos, k_pos, lam):
    B, Lq = q.shape[:2]
    Lk = k.shape[1]
    qh = q.reshape(B, Lq, DA_HEADS, 2, DA_DK)
    kh = k.reshape(B, Lk, DA_HEADS, 2, DA_DK)
    s = jnp.einsum('bqhcd,bkhcd->bhcqk', qh, kh).astype(jnp.float32)
    mask = k_pos[None, :] <= q_pos[:, None]
    s = jnp.where(mask, s, jnp.float32(NEG_INF))
    p = jax.nn.softmax(s, axis=-1)
    a = p[:, :, 0] - lam * p[:, :, 1]
    return jnp.einsum('bhqk,bkhv->bqhv', a.astype(v.dtype), v)


def diff_attn_prompt(q, k, v, lam):
    B, T = q.shape[:2]
    k_pos = jnp.arange(T)

    def block(i):
        start = i * Q_BLOCK
        qb = lax.dynamic_slice_in_dim(q, start, Q_BLOCK, axis=1)
        return diff_attend(qb, k, v, start + jnp.arange(Q_BLOCK), k_pos, lam)

    o = lax.map(block, jnp.arange(T // Q_BLOCK))
    return o.swapaxes(0, 1).reshape(B, T, DA_HEADS, DA_DV)


def diff_attn_cached(q, k, v, lam, past_k, past_v, q_pos):
    k_all = jnp.concatenate([past_k.astype(k.dtype), k], axis=1)
    v_all = jnp.concatenate([past_v.astype(v.dtype), v], axis=1)
    k_pos = jnp.arange(k_all.shape[1])
    return diff_attend(q, k_all, v_all, q_pos, k_pos, lam)


def token_mixer(h, pos, lam_init, attn_fn, ret_fn, w_in, w_out, lam_q1, lam_k1, lam_q2, lam_k2,
                subln_g, ret_norm_g):
    B, T, _ = h.shape
    split_at = [int(s) for s in np.cumsum(IN_WIDTHS)[:-1]]
    qa, ka, va, qr, kr, vr, gr, ga, gb = jnp.split(h @ w_in, split_at, axis=-1)
    qa = rope_half(qa.reshape(B, T, 2 * DA_HEADS, DA_DK), pos) * (DA_DK ** -0.5)
    ka = rope_half(ka.reshape(B, T, 2 * DA_HEADS, DA_DK), pos)
    va = va.reshape(B, T, DA_HEADS, DA_DV)
    lam = (jnp.exp(jnp.sum(lam_q1.astype(jnp.float32) * lam_k1.astype(jnp.float32)))
           - jnp.exp(jnp.sum(lam_q2.astype(jnp.float32) * lam_k2.astype(jnp.float32))) + lam_init)
    o_a = attn_fn(qa, ka, va, lam)
    o_a = (rms_norm(o_a, subln_g, SUBLN_EPS) * (1.0 - lam_init)).reshape(B, T, D_MODEL)
    qr = retnet_rotate(qr.reshape(B, T, RET_HEADS, RET_DK), pos)
    kr = retnet_rotate(kr.reshape(B, T, RET_HEADS, RET_DK), pos) * (RET_DK ** -0.5)
    vr = vr.reshape(B, T, RET_HEADS, RET_DV)
    o_r, ret_state = ret_fn(qr, kr, vr)
    o_r = group_norm_heads(o_r, ret_norm_g).reshape(B, T, D_MODEL).astype(h.dtype) * jax.nn.silu(gr)
    merged = jax.nn.sigmoid(ga) * o_a + jax.nn.sigmoid(gb) * o_r
    return merged @ w_out, (ka, va, ret_state)


def decoder_layer(x, c, pos, lam_init, attn_fn, ret_fn, ada_w, ada_b, norm_ffn1, norm_mix, norm_ffn2,
                  ffn1_w_in, ffn1_w_out, ffn2_w_in, ffn2_w_out, w_in, w_out, lam_q1, lam_k1, lam_q2,
                  lam_k2, subln_g, ret_norm_g):
    mod = jax.nn.silu(c) @ ada_w + ada_b
    sh1, sc1, gt1, sh2, sc2, gt2, sh3, sc3, gt3 = jnp.split(mod, N_MOD, axis=-1)
    h = modulate(x, norm_ffn1, sh1, sc1)
    x = x + HALF_STEP * gt1[:, None, :] * swiglu_ffn(h, ffn1_w_in, ffn1_w_out)
    h = modulate(x, norm_mix, sh2, sc2)
    m, new_state = token_mixer(h, pos, lam_init, attn_fn, ret_fn, w_in, w_out, lam_q1, lam_k1,
                               lam_q2, lam_k2, subln_g, ret_norm_g)
    x = x + gt2[:, None, :] * m
    h = modulate(x, norm_ffn2, sh3, sc3)
    x = x + HALF_STEP * gt3[:, None, :] * swiglu_ffn(h, ffn2_w_in, ffn2_w_out)
    return x, new_state


def setup_inputs(seed: int = 0) -> dict:
    key = jax.random.key(seed)
    ks = jax.random.split(key, 32)
    f32 = jnp.float32

    def w(k, shape, fan_in):
        return jax.random.normal(k, shape, f32) * (fan_in ** -0.5)

    def gain(k, shape):
        return 1.0 + 0.05 * jax.random.normal(k, shape, f32)

    n_pages = PAST_LEN // PAGE_SIZE
    n_used = DEC_BATCH * n_pages
    n_pool = n_used + (n_used + 3) // 4
    page_table = jax.random.permutation(ks[0], n_pool)[:n_used].reshape(DEC_BATCH, n_pages).astype(jnp.int32)
    return {
        'x_prompt': jax.random.normal(ks[1], (BATCH, SEQ, D_MODEL), f32),
        'x_sample': jax.random.normal(ks[2], (DEC_BATCH, DEC_SEQ, D_MODEL), f32),
        'cache_k': jax.random.normal(ks[3], (DEPTH, n_pool, PAGE_SIZE, 2 * DA_HEADS, DA_DK), f32),
        'cache_v': jax.random.normal(ks[4], (DEPTH, n_pool, PAGE_SIZE, DA_HEADS, DA_DV), f32),
        'state_ret': 0.5 * jax.random.normal(ks[5], (DEPTH, DEC_BATCH, RET_HEADS, RET_DK, RET_DV), f32),
        'page_table': page_table,
        'c_prompt': jax.random.normal(ks[6], (BATCH, D_MODEL), f32),
        'c_sample': jax.random.normal(ks[7], (DEC_BATCH, D_MODEL), f32),
        'ada_w': w(ks[8], (DEPTH, D_MODEL, N_MOD * D_MODEL), D_MODEL),
        'ada_b': 0.01 * jax.random.normal(ks[9], (DEPTH, N_MOD * D_MODEL), f32),
        'norm_ffn1': gain(ks[10], (DEPTH, D_MODEL)),
        'norm_mix': gain(ks[11], (DEPTH, D_MODEL)),
        'norm_ffn2': gain(ks[12], (DEPTH, D_MODEL)),
        'ffn1_w_in': w(ks[13], (DEPTH, D_MODEL, 2 * D_FF), D_MODEL),
        'ffn1_w_out': w(ks[14], (DEPTH, D_FF, D_MODEL), D_FF),
        'ffn2_w_in': w(ks[15], (DEPTH, D_MODEL, 2 * D_FF), D_MODEL),
        'ffn2_w_out': w(ks[16], (DEPTH, D_FF, D_MODEL), D_FF),
        'w_in': w(ks[17], (DEPTH, D_MODEL, W_IN_COLS), D_MODEL),
        'w_out': w(ks[18], (DEPTH, D_MODEL, D_MODEL), D_MODEL),
        'lam_q1': 0.1 * jax.random.normal(ks[19], (DEPTH, DA_DK), f32),
        'lam_k1': 0.1 * jax.random.normal(ks[20], (DEPTH, DA_DK), f32),
        'lam_q2': 0.1 * jax.random.normal(ks[21], (DEPTH, DA_DK), f32),
        'lam_k2': 0.1 * jax.random.normal(ks[22], (DEPTH, DA_DK), f32),
        'subln_g': gain(ks[23], (DEPTH, DA_DV)),
        'ret_norm_g': gain(ks[24], (DEPTH, D_MODEL)),
        'norm_final': gain(ks[25], (D_MODEL,)),
    }


def reference(x_prompt, x_sample, cache_k, cache_v, state_ret, page_table, c_prompt, c_sample,
              ada_w, ada_b, norm_ffn1, norm_mix, norm_ffn2, ffn1_w_in, ffn1_w_out, ffn2_w_in,
              ffn2_w_out, w_in, w_out, lam_q1, lam_k1, lam_q2, lam_k2, subln_g, ret_norm_g,
              norm_final):
    seq = x_prompt.shape[1]
    dec_batch, dec_seq = x_sample.shape[:2]
    past_len = page_table.shape[1] * cache_k.shape[2]
    pos_prompt = jnp.arange(seq)
    pos_sample = past_len + jnp.arange(dec_seq)
    log_g = retention_log_decay()
    ret_prompt_fn = functools.partial(retention_prompt, log_g=log_g)

    yp, ys = x_prompt, x_sample
    kp_l, vp_l, sp_l, ks_l, vs_l, ss_l = [], [], [], [], [], []
    for l in range(DEPTH):
        lam_init = 0.8 - 0.6 * math.exp(-0.3 * l)
        weights = (ada_w[l], ada_b[l], norm_ffn1[l], norm_mix[l], norm_ffn2[l], ffn1_w_in[l],
                   ffn1_w_out[l], ffn2_w_in[l], ffn2_w_out[l], w_in[l], w_out[l], lam_q1[l],
                   lam_k1[l], lam_q2[l], lam_k2[l], subln_g[l], ret_norm_g[l])
        yp, (kp, vp, sp) = decoder_layer(yp, c_prompt, pos_prompt, lam_init, diff_attn_prompt,
                                         ret_prompt_fn, *weights)
        past_k = cache_k[l][page_table].reshape(dec_batch, past_len, 2 * DA_HEADS, DA_DK)
        past_v = cache_v[l][page_table].reshape(dec_batch, past_len, DA_HEADS, DA_DV)
        attn_s = functools.partial(diff_attn_cached, past_k=past_k, past_v=past_v, q_pos=pos_sample)
        ret_s = functools.partial(retention_chunk, state=state_ret[l].astype(jnp.float32), log_g=log_g)
        ys, (ksm, vsm, ssm) = decoder_layer(ys, c_sample, pos_sample, lam_init, attn_s, ret_s, *weights)
        kp_l.append(kp)
        vp_l.append(vp)
        sp_l.append(sp)
        ks_l.append(ksm)
        vs_l.append(vsm)
        ss_l.append(ssm)

    y_prompt = rms_norm(yp, norm_final)
    y_sample = rms_norm(ys, norm_final)
    k_prompt = jnp.stack(kp_l)
    v_prompt = jnp.stack(vp_l)
    state_ret_prompt = jnp.stack(sp_l)
    k_sample = jnp.stack(ks_l)
    v_sample = jnp.stack(vs_l)
    state_ret_sample = jnp.stack(ss_l)
    return (y_prompt, y_sample, k_prompt, v_prompt, state_ret_prompt, k_sample, v_sample, state_ret_sample)
```

```python
import functools
import math

import jax
import jax.numpy as jnp
from jax import lax
from jax.experimental import pallas as pl
from jax.experimental.pallas import tpu as pltpu

F32 = jnp.float32
BF16 = jnp.bfloat16

D_MODEL = 1024
DA_DK = 64
DA_DV = 128
DA_HEADS = 8
RET_HEADS = 4
RET_DK = 256
RET_DV = 256
D_FF = 2816
N_MOD = 9
N_PROJ = 9
HALF_STEP = 0.5
ROPE_THETA = 10000.0
RET_THETA = 10000.0
NORM_EPS = 1e-6
SUBLN_EPS = 1e-5
GN_EPS = 1e-5
NEG_INF = -1e30
LOG2E = 1.4426950408889634
LANES = 128
VMEM_LIMIT = 56 * 1024 * 1024

FFN_TM = 512
FFN_TF = 1408
PROJ_TM = 256
ATTN_TQ = 512
RET_L = 512
MERGE_TM = 512


def _cparams(sem):
    return pltpu.CompilerParams(dimension_semantics=sem, vmem_limit_bytes=VMEM_LIMIT)


def _rms(x, g, eps):
    return x * lax.rsqrt(jnp.mean(x * x, axis=-1, keepdims=True) + eps) * g


def _mod_spec(kind, tm, rows_per_mod, piece):
    if kind == "batch":
        return pl.BlockSpec((None, 1, D_MODEL), lambda i, *_: ((i * tm) // rows_per_mod, 0, piece))
    return pl.BlockSpec((tm, D_MODEL), lambda i, *_: (i, piece))


def _adaln_kernel(c_ref, w_ref, b_ref, o_ref):
    c = c_ref[...]
    a = (c * jax.nn.sigmoid(c)).astype(BF16)
    o_ref[...] = jnp.dot(a, w_ref[...].astype(BF16), preferred_element_type=F32) + b_ref[...]


def _adaln(c, w, b):
    m = c.shape[0]
    n = w.shape[1]
    tn = 1152
    return pl.pallas_call(
        _adaln_kernel,
        grid=(n // tn,),
        in_specs=[
            pl.BlockSpec((m, D_MODEL), lambda j: (0, 0)),
            pl.BlockSpec((D_MODEL, tn), lambda j: (0, j)),
            pl.BlockSpec((1, tn), lambda j: (0, j)),
        ],
        out_specs=pl.BlockSpec((m, tn), lambda j: (0, j)),
        out_shape=jax.ShapeDtypeStruct((m, n), F32),
        compiler_params=_cparams(("arbitrary",)),
        name="adaln",
    )(c, w, b)


def _ffn_kernel(x_ref, sh_ref, sc_ref, gt_ref, g_ref, wa_ref, wb_ref, wo_ref, nf_ref, o_ref, h_ref, acc_ref,
                *, n_ff, final_norm):
    j = pl.program_id(1)

    @pl.when(j == 0)
    def _():
        h = _rms(x_ref[...], g_ref[...], NORM_EPS) * (1.0 + sc_ref[...]) + sh_ref[...]
        h_ref[...] = h.astype(BF16)

    h = h_ref[...]
    a = jnp.dot(h, wa_ref[...], preferred_element_type=F32)
    b = jnp.dot(h, wb_ref[...], preferred_element_type=F32)
    act = (a * jax.nn.sigmoid(a) * b).astype(BF16)
    part = jnp.dot(act, wo_ref[...], preferred_element_type=F32)

    @pl.when(j == 0)
    def _():
        acc_ref[...] = part

    @pl.when(j > 0)
    def _():
        acc_ref[...] += part

    @pl.when(j == n_ff - 1)
    def _():
        out = x_ref[...] + HALF_STEP * gt_ref[...] * acc_ref[...]
        if final_norm:
            out = _rms(out, nf_ref[...], NORM_EPS)
        o_ref[...] = out


def _ffn(x, mod, mod_kind, rows_per_mod, pieces, norm_g, w_in, w_out, norm_final, final_norm):
    rows = x.shape[0]
    tm = min(FFN_TM, rows)
    tf = FFN_TF
    n_ff = D_FF // tf
    row = lambda i, j: (i, 0)
    const = lambda i, j: (0, 0)
    return pl.pallas_call(
        functools.partial(_ffn_kernel, n_ff=n_ff, final_norm=final_norm),
        grid=(rows // tm, n_ff),
        in_specs=[
            pl.BlockSpec((tm, D_MODEL), row),
            _mod_spec(mod_kind, tm, rows_per_mod, pieces[0]),
            _mod_spec(mod_kind, tm, rows_per_mod, pieces[1]),
            _mod_spec(mod_kind, tm, rows_per_mod, pieces[2]),
            pl.BlockSpec((1, D_MODEL), const),
            pl.BlockSpec((D_MODEL, tf), lambda i, j: (0, j)),
            pl.BlockSpec((D_MODEL, tf), lambda i, j: (0, n_ff + j)),
            pl.BlockSpec((tf, D_MODEL), lambda i, j: (j, 0)),
            pl.BlockSpec((1, D_MODEL), const),
        ],
        out_specs=pl.BlockSpec((tm, D_MODEL), row),
        out_shape=jax.ShapeDtypeStruct((rows, D_MODEL), F32),
        scratch_shapes=[pltpu.VMEM((tm, D_MODEL), BF16), pltpu.VMEM((tm, D_MODEL), F32)],
        compiler_params=_cparams(("parallel", "arbitrary")),
        name="ffn",
    )(x, mod, mod, mod, norm_g, w_in, w_in, w_out, norm_final)


def _table_kernel(inv_r_ref, sgn_r_ref, inv_t_ref, sgn_t_ref, cr_ref, sr_ref, ct_ref, st_ref, *, offset):
    rows = cr_ref.shape[0]
    pos = (lax.broadcasted_iota(jnp.int32, (rows, 1), 0) + (pl.program_id(0) * rows + offset)).astype(F32)
    ang = pos * inv_r_ref[...]
    cr_ref[...] = jnp.cos(ang)
    sr_ref[...] = jnp.sin(ang) * sgn_r_ref[...]
    ang = pos * inv_t_ref[...]
    ct_ref[...] = jnp.cos(ang)
    st_ref[...] = jnp.sin(ang) * sgn_t_ref[...]


def _rotation_tables(n_pos, offset):
    lane = jnp.arange(LANES)
    half = DA_DK // 2
    inv_r = (ROPE_THETA ** (-((lane % DA_DK) % half).astype(F32) / half))[None, :]
    sgn_r = jnp.where((lane % DA_DK) < half, -1.0, 1.0).astype(F32)[None, :]
    lane_t = jnp.arange(RET_DK)
    angle = 1.0 / (RET_THETA ** jnp.linspace(0.0, 1.0, RET_DK // 2, dtype=F32))
    inv_t = angle[lane_t // 2][None, :]
    sgn_t = jnp.where(lane_t % 2 == 0, -1.0, 1.0).astype(F32)[None, :]
    rows = min(n_pos, 512)
    vec = lambda w: pl.BlockSpec((1, w), lambda i: (0, 0))
    tab = lambda w: pl.BlockSpec((rows, w), lambda i: (i, 0))
    return pl.pallas_call(
        functools.partial(_table_kernel, offset=offset),
        grid=(n_pos // rows,),
        in_specs=[vec(LANES), vec(LANES), vec(RET_DK), vec(RET_DK)],
        out_specs=[tab(LANES), tab(LANES), tab(RET_DK), tab(RET_DK)],
        out_shape=[jax.ShapeDtypeStruct((n_pos, LANES), F32), jax.ShapeDtypeStruct((n_pos, LANES), F32),
                   jax.ShapeDtypeStruct((n_pos, RET_DK), F32), jax.ShapeDtypeStruct((n_pos, RET_DK), F32)],
        compiler_params=_cparams(("arbitrary",)),
        name="rotation_tables",
    )(inv_r, sgn_r, inv_t, sgn_t)


def _pair_rotate(x, cos, sin_signed, shift):
    lane = lax.broadcasted_iota(jnp.int32, x.shape, 1)
    partner = jnp.where((lane % (2 * shift)) < shift, pltpu.roll(x, LANES - shift, 1), pltpu.roll(x, shift, 1))
    return x * cos + partner * sin_signed


def _proj_kernel(x_ref, sh_ref, sc_ref, g_ref, w_ref, cr_ref, sr_ref, ct_ref, st_ref, *out_refs, transposed):
    h = (_rms(x_ref[...], g_ref[...], NORM_EPS) * (1.0 + sc_ref[...]) + sh_ref[...]).astype(BF16)
    cr, sr = cr_ref[...], sr_ref[...]
    n_chunks = D_MODEL // LANES
    q_scale = (DA_DK ** -0.5) * LOG2E
    k_scale = RET_DK ** -0.5

    def group(gi):
        return jnp.dot(h, w_ref[:, gi * D_MODEL:(gi + 1) * D_MODEL], preferred_element_type=F32)

    def chunk(p, c):
        return p[:, c * LANES:(c + 1) * LANES]

    def ret_tab(t_ref, c):
        half = (c % (RET_DK // LANES)) * LANES
        return t_ref[:, half:half + LANES]

    if transposed:
        (q_ref, ktf_ref, ktb_ref, vf_ref, vb_ref, qr_ref, krt_ref, vr_ref, gr_ref, ga_ref, gb_ref) = out_refs
    else:
        (q_ref, k_ref, v_ref, qr_ref, kr_ref, vr_ref, gr_ref, ga_ref, gb_ref) = out_refs

    p = group(0)
    for c in range(n_chunks):
        q_ref[:, c * LANES:(c + 1) * LANES] = (_pair_rotate(chunk(p, c), cr, sr, DA_DK // 2) * q_scale).astype(q_ref.dtype)
    p = group(1)
    for c in range(n_chunks):
        kc = _pair_rotate(chunk(p, c), cr, sr, DA_DK // 2)
        if transposed:
            kt = kc.T
            ktf_ref[c * LANES:(c + 1) * LANES, :] = kt
            ktb_ref[c * LANES:(c + 1) * LANES, :] = kt.astype(BF16)
        else:
            k_ref[:, c * LANES:(c + 1) * LANES] = kc
    p = group(2)
    if transposed:
        vf_ref[...] = p
        vb_ref[...] = p.astype(BF16)
    else:
        v_ref[...] = p
    p = group(3)
    for c in range(n_chunks):
        qc = _pair_rotate(chunk(p, c), ret_tab(ct_ref, c), ret_tab(st_ref, c), 1)
        qr_ref[:, c * LANES:(c + 1) * LANES] = qc.astype(qr_ref.dtype)
    p = group(4)
    for c in range(n_chunks):
        kc = _pair_rotate(chunk(p, c), ret_tab(ct_ref, c), ret_tab(st_ref, c), 1) * k_scale
        if transposed:
            krt_ref[c * LANES:(c + 1) * LANES, :] = kc.T.astype(BF16)
        else:
            kr_ref[:, c * LANES:(c + 1) * LANES] = kc
    for gi, ref in ((5, vr_ref), (6, gr_ref), (7, ga_ref), (8, gb_ref)):
        ref[...] = group(gi).astype(ref.dtype)


def _proj(x, mod, mod_kind, rows_per_mod, norm_g, w_in, tables, n_tab_tiles, transposed, batch, seq):
    rows = x.shape[0]
    tm = min(PROJ_TM, rows)
    row = lambda i: (i, 0)
    const = lambda i: (0, 0)
    tab = lambda w: pl.BlockSpec((tm, w), lambda i: (i % n_tab_tiles, 0))
    rowspec = pl.BlockSpec((tm, D_MODEL), row)
    if transposed:
        tiles = seq // tm
        tspec = pl.BlockSpec((None, D_MODEL, tm), lambda i: (i // tiles, 0, i % tiles))
        t_shape = lambda dt: jax.ShapeDtypeStruct((batch, D_MODEL, seq), dt)
        r_shape = lambda dt: jax.ShapeDtypeStruct((rows, D_MODEL), dt)
        out_specs = [rowspec, tspec, tspec, rowspec, rowspec, rowspec, tspec, rowspec, rowspec, rowspec, rowspec]
        out_shape = [r_shape(BF16), t_shape(F32), t_shape(BF16), r_shape(F32), r_shape(BF16), r_shape(BF16),
                     t_shape(BF16), r_shape(BF16), r_shape(BF16), r_shape(BF16), r_shape(BF16)]
    else:
        out_specs = [rowspec] * N_PROJ
        out_shape = [jax.ShapeDtypeStruct((rows, D_MODEL), F32)] * N_PROJ
    return pl.pallas_call(
        functools.partial(_proj_kernel, transposed=transposed),
        grid=(rows // tm,),
        in_specs=[
            rowspec,
            _mod_spec(mod_kind, tm, rows_per_mod, 3),
            _mod_spec(mod_kind, tm, rows_per_mod, 4),
            pl.BlockSpec((1, D_MODEL), const),
            pl.BlockSpec((D_MODEL, N_PROJ * D_MODEL), const, pipeline_mode=pl.Buffered(1)),
            tab(LANES), tab(LANES), tab(RET_DK), tab(RET_DK),
        ],
        out_specs=out_specs,
        out_shape=out_shape,
        compiler_params=_cparams(("parallel",)),
        name="mixer_proj",
    )(x, mod, mod, norm_g, w_in, *tables)


def _lambda(lq1, lk1, lq2, lk2, lam_init):
    return (jnp.exp(jnp.sum(lq1 * lk1, axis=-1, keepdims=True))
            - jnp.exp(jnp.sum(lq2 * lk2, axis=-1, keepdims=True)) + lam_init)


def _attn_kernel(qi_ref, kj_ref, q_ref, kt_ref, v_ref, lq1_ref, lk1_ref, lq2_ref, lk2_ref, sg_ref, o_ref,
                 qz_ref, m_ref, l_ref, acc_ref, *, tq, lam_init):
    step = pl.program_id(1)
    qi = qi_ref[step]
    kj = kj_ref[step]
    lane = lax.broadcasted_iota(jnp.int32, (tq, DA_DV), 1)

    @pl.when(kj == 0)
    def _():
        for h in range(DA_HEADS):
            qh = q_ref[:, h * DA_DV:(h + 1) * DA_DV]
            qz_ref[h, :tq, :] = jnp.where(lane < DA_DK, qh, jnp.zeros_like(qh))
            qz_ref[h, tq:, :] = jnp.where(lane >= DA_DK, qh, jnp.zeros_like(qh))
        m_ref[...] = jnp.full(m_ref.shape, NEG_INF, F32)
        l_ref[...] = jnp.zeros(l_ref.shape, F32)
        acc_ref[...] = jnp.zeros(acc_ref.shape, F32)

    def block(masked):
        if masked:
            r = lax.broadcasted_iota(jnp.int32, (2 * tq, tq), 0) % tq
            c = lax.broadcasted_iota(jnp.int32, (2 * tq, tq), 1)
            keep = c <= r
        for h in range(DA_HEADS):
            s = jnp.dot(qz_ref[h], kt_ref[h * DA_DV:(h + 1) * DA_DV, :], preferred_element_type=F32)
            if masked:
                s = jnp.where(keep, s, NEG_INF)
            m_old = m_ref[h]
            m_new = jnp.maximum(m_old, jnp.max(s, axis=-1, keepdims=True))
            alpha = jnp.exp2(m_old - m_new)
            p = jnp.exp2(s - m_new)
            l_ref[h] = alpha * l_ref[h] + jnp.sum(p, axis=-1, keepdims=True)
            acc_ref[h] = alpha * acc_ref[h] + jnp.dot(p.astype(BF16), v_ref[:, h * DA_DV:(h + 1) * DA_DV],
                                                      preferred_element_type=F32)
            m_ref[h] = m_new

    @pl.when(kj < qi)
    def _():
        block(False)

    @pl.when(kj == qi)
    def _():
        block(True)
        lam = _lambda(lq1_ref[...], lk1_ref[...], lq2_ref[...], lk2_ref[...], lam_init)
        for h in range(DA_HEADS):
            o = acc_ref[h] / l_ref[h]
            d = o[:tq] - lam * o[tq:]
            o_ref[:, h * DA_DV:(h + 1) * DA_DV] = (_rms(d, sg_ref[...], SUBLN_EPS) * (1.0 - lam_init)).astype(o_ref.dtype)


def _attn_prompt(q, kt, v, lam_vecs, subln_g, batch, seq, lam_init):
    tq = min(ATTN_TQ, seq)
    nq = seq // tq
    pairs = [(i, j) for i in range(nq) for j in range(i + 1)]
    qi = jnp.asarray([p[0] for p in pairs], jnp.int32)
    kj = jnp.asarray([p[1] for p in pairs], jnp.int32)
    vec = lambda w: pl.BlockSpec((1, w), lambda b, s, qi, kj: (0, 0))
    grid_spec = pltpu.PrefetchScalarGridSpec(
        num_scalar_prefetch=2,
        grid=(batch, len(pairs)),
        in_specs=[
            pl.BlockSpec((tq, D_MODEL), lambda b, s, qi, kj: (b * nq + qi[s], 0)),
            pl.BlockSpec((None, D_MODEL, tq), lambda b, s, qi, kj: (b, 0, kj[s])),
            pl.BlockSpec((tq, D_MODEL), lambda b, s, qi, kj: (b * nq + kj[s], 0)),
            vec(DA_DK), vec(DA_DK), vec(DA_DK), vec(DA_DK), vec(DA_DV),
        ],
        out_specs=pl.BlockSpec((tq, D_MODEL), lambda b, s, qi, kj: (b * nq + qi[s], 0)),
        scratch_shapes=[
            pltpu.VMEM((DA_HEADS, 2 * tq, DA_DV), BF16),
            pltpu.VMEM((DA_HEADS, 2 * tq, 1), F32),
            pltpu.VMEM((DA_HEADS, 2 * tq, 1), F32),
            pltpu.VMEM((DA_HEADS, 2 * tq, DA_DV), F32),
        ],
    )
    return pl.pallas_call(
        functools.partial(_attn_kernel, tq=tq, lam_init=lam_init),
        grid_spec=grid_spec,
        out_shape=jax.ShapeDtypeStruct((batch * seq, D_MODEL), BF16),
        compiler_params=_cparams(("parallel", "arbitrary")),
        name="diff_attn_prompt",
    )(qi, kj, q, kt, v, *lam_vecs, subln_g)


def _attn_sample_kernel(pt_ref, *refs, n_pages, page, dec_seq, lam_init):
    k_refs = refs[:n_pages]
    v_refs = refs[n_pages:2 * n_pages]
    (q_ref, kn_ref, vn_ref, lq1_ref, lk1_ref, lq2_ref, lk2_ref, sg_ref, o_ref, s_ref) = refs[2 * n_pages:]
    del pt_ref
    rows = 2 * dec_seq
    q = q_ref[...].astype(BF16)
    for i in range(n_pages):
        s_ref[:, :, i * page:(i + 1) * page] = jnp.einsum(
            "hrk,hkt->hrt", q, k_refs[i][...].astype(BF16), preferred_element_type=F32)
    qf = q.astype(F32)
    kn = kn_ref[...].astype(BF16).astype(F32)
    vn = vn_ref[...].astype(BF16).astype(F32)
    r = lax.broadcasted_iota(jnp.int32, (DA_HEADS, rows, 1), 1) % dec_seq
    s_new = [jnp.where(r >= t, jnp.sum(qf * kn[:, t:t + 1, :], axis=-1, keepdims=True), NEG_INF)
             for t in range(dec_seq)]
    s_old = s_ref[...]
    m = jnp.max(s_old, axis=-1, keepdims=True)
    for s_t in s_new:
        m = jnp.maximum(m, s_t)
    p_old = jnp.exp2(s_old - m)
    denom = jnp.sum(p_old, axis=-1, keepdims=True)
    acc_new = jnp.zeros((DA_HEADS, rows, DA_DV), F32)
    for t, s_t in enumerate(s_new):
        p_t = jnp.exp2(s_t - m)
        denom = denom + p_t
        acc_new = acc_new + p_t.astype(BF16).astype(F32) * vn[:, t:t + 1, :]
    s_ref[...] = p_old
    lam = _lambda(lq1_ref[...], lk1_ref[...], lq2_ref[...], lk2_ref[...], lam_init)
    for h in range(DA_HEADS):
        acc = acc_new[h]
        for i in range(n_pages):
            acc = acc + jnp.dot(s_ref[h, :, i * page:(i + 1) * page].astype(BF16),
                                v_refs[i][:, h, :].astype(BF16), preferred_element_type=F32)
        o = acc / denom[h]
        d = o[:dec_seq] - lam * o[dec_seq:]
        o_ref[:, h * DA_DV:(h + 1) * DA_DV] = _rms(d, sg_ref[...], SUBLN_EPS) * (1.0 - lam_init)


def _attn_sample(page_table, cache_kt, cache_v, q_bd, k_new, v_new, lam_vecs, subln_g, lam_init):
    dec_batch, n_pages = page_table.shape
    page = cache_v.shape[1]
    dec_seq = k_new.shape[2]
    kspec = lambda i: pl.BlockSpec((None, DA_HEADS, DA_DV, page), lambda b, pt: (pt[b, i], 0, 0, 0))
    vspec = lambda i: pl.BlockSpec((None, page, DA_HEADS, DA_DV), lambda b, pt: (pt[b, i], 0, 0, 0))
    per_b = lambda r: pl.BlockSpec((None, DA_HEADS, r, DA_DV), lambda b, pt: (b, 0, 0, 0))
    vec = lambda w: pl.BlockSpec((1, w), lambda b, pt: (0, 0))
    grid_spec = pltpu.PrefetchScalarGridSpec(
        num_scalar_prefetch=1,
        grid=(dec_batch,),
        in_specs=[kspec(i) for i in range(n_pages)] + [vspec(i) for i in range(n_pages)] + [
            per_b(2 * dec_seq), per_b(dec_seq), per_b(dec_seq),
            vec(DA_DK), vec(DA_DK), vec(DA_DK), vec(DA_DK), vec(DA_DV)],
        out_specs=pl.BlockSpec((None, dec_seq, D_MODEL), lambda b, pt: (b, 0, 0)),
        scratch_shapes=[pltpu.VMEM((DA_HEADS, 2 * dec_seq, n_pages * page), F32)],
    )
    return pl.pallas_call(
        functools.partial(_attn_sample_kernel, n_pages=n_pages, page=page, dec_seq=dec_seq, lam_init=lam_init),
        grid_spec=grid_spec,
        out_shape=jax.ShapeDtypeStruct((dec_batch, dec_seq, D_MODEL), F32),
        compiler_params=_cparams(("parallel",)),
        name="diff_attn_sample",
    )(page_table, *([cache_kt] * n_pages), *([cache_v] * n_pages), q_bd, k_new, v_new, *lam_vecs, subln_g)


def _group_norm(o, g):
    mu = jnp.mean(o, axis=-1, keepdims=True)
    d = o - mu
    return d * lax.rsqrt(jnp.mean(d * d, axis=-1, keepdims=True) + GN_EPS) * g


def _ret_prompt_kernel(lg_ref, q_ref, kt_ref, v_ref, g_ref, o_ref, st_ref, state_ref, *, chunk):
    h = pl.program_id(1)
    c = pl.program_id(2)
    lg = lg_ref[h]

    @pl.when(c == 0)
    def _():
        state_ref[...] = jnp.zeros(state_ref.shape, F32)

    q = q_ref[...]
    kt = kt_ref[...]
    v = v_ref[...]
    ri = lax.broadcasted_iota(jnp.int32, (chunk, chunk), 0)
    ci = lax.broadcasted_iota(jnp.int32, (chunk, chunk), 1)
    dist = (ri - ci).astype(F32)
    decay = jnp.where(dist >= 0, jnp.exp(jnp.maximum(dist, 0.0) * lg), 0.0)
    scores = jnp.dot(q, kt, preferred_element_type=F32) * decay
    inner = jnp.dot(scores.astype(BF16), v, preferred_element_type=F32)
    q_decay = jnp.exp((lax.broadcasted_iota(jnp.int32, (chunk, 1), 0) + 1).astype(F32) * lg)
    state = state_ref[...]
    cross = jnp.dot(q, state.astype(BF16), preferred_element_type=F32) * q_decay
    o_ref[...] = _group_norm(inner + cross, g_ref[...]).astype(o_ref.dtype)
    k_decay = jnp.exp((chunk - 1 - lax.broadcasted_iota(jnp.int32, (1, chunk), 1)).astype(F32) * lg)
    ktd = (kt.astype(F32) * k_decay).astype(BF16)
    new_state = jnp.exp(jnp.full((1, 1), chunk, F32) * lg) * state + jnp.dot(ktd, v, preferred_element_type=F32)
    state_ref[...] = new_state

    @pl.when(c == pl.num_programs(2) - 1)
    def _():
        st_ref[...] = new_state


def _ret_prompt(log_g, q, kt, v, gn_g, batch, seq):
    chunk = min(RET_L, seq)
    nc = seq // chunk
    rowspec = pl.BlockSpec((chunk, RET_DV), lambda b, h, c: (b * nc + c, h))
    return pl.pallas_call(
        functools.partial(_ret_prompt_kernel, chunk=chunk),
        grid=(batch, RET_HEADS, nc),
        in_specs=[
            pl.BlockSpec(memory_space=pltpu.SMEM),
            rowspec,
            pl.BlockSpec((None, RET_DK, chunk), lambda b, h, c: (b, h, c)),
            rowspec,
            pl.BlockSpec((1, RET_DV), lambda b, h, c: (0, h)),
        ],
        out_specs=[rowspec, pl.BlockSpec((None, None, RET_DK, RET_DV), lambda b, h, c: (b, h, 0, 0))],
        out_shape=[jax.ShapeDtypeStruct((batch * seq, D_MODEL), BF16),
                   jax.ShapeDtypeStruct((batch, RET_HEADS, RET_DK, RET_DV), F32)],
        scratch_shapes=[pltpu.VMEM((RET_DK, RET_DV), F32)],
        compiler_params=_cparams(("parallel", "parallel", "arbitrary")),
        name="retention_prompt",
    )(log_g, q, kt, v, gn_g)


def _ret_sample_kernel(lg_ref, q_ref, k_ref, v_ref, s_ref, g_ref, o_ref, st_ref, *, dec_seq):
    rows = q_ref.shape[0]
    ri = lax.broadcasted_iota(jnp.int32, (rows, 1), 0)
    for h in range(RET_HEADS):
        lg = lg_ref[h]
        sl = slice(h * RET_DK, (h + 1) * RET_DK)
        q = q_ref[:, sl]
        k = k_ref[:, sl]
        v = v_ref[:, sl]
        qb = q.astype(BF16).astype(F32)
        kb = k.astype(BF16).astype(F32)
        vb = v.astype(BF16).astype(F32)
        state = s_ref[h]
        inner = jnp.zeros((rows, RET_DV), F32)
        for j in range(dec_seq):
            dist = (ri - j).astype(F32)
            decay = jnp.where(dist >= 0, jnp.exp(jnp.maximum(dist, 0.0) * lg), 0.0)
            score = jnp.sum(qb * kb[j:j + 1, :], axis=-1, keepdims=True) * decay
            inner = inner + score.astype(BF16).astype(F32) * vb[j:j + 1, :]
        q_decay = jnp.exp((ri + 1).astype(F32) * lg)
        cross = jnp.dot(q.astype(BF16), state.astype(BF16), preferred_element_type=F32) * q_decay
        o_ref[:, sl] = _group_norm(inner + cross, g_ref[:, sl])
        k_decay = jnp.where(ri < dec_seq, jnp.exp((dec_seq - 1 - ri).astype(F32) * lg), 0.0)
        zpad = jnp.zeros((LANES - rows, RET_DK), F32)
        kd_t = jnp.concatenate([k * k_decay, zpad], axis=0).T.astype(BF16)
        v_pad = jnp.concatenate([v, zpad], axis=0).astype(BF16)
        upd = jnp.dot(kd_t, v_pad, preferred_element_type=F32)
        st_ref[h] = jnp.exp(jnp.full((1, 1), dec_seq, F32) * lg) * state + upd


def _ret_sample(log_g, q, k, v, state, gn_g, dec_seq):
    dec_batch, rows = q.shape[:2]
    tok = pl.BlockSpec((None, rows, D_MODEL), lambda b: (b, 0, 0))
    st = pl.BlockSpec((None, RET_HEADS, RET_DK, RET_DV), lambda b: (b, 0, 0, 0))
    return pl.pallas_call(
        functools.partial(_ret_sample_kernel, dec_seq=dec_seq),
        grid=(dec_batch,),
        in_specs=[pl.BlockSpec(memory_space=pltpu.SMEM), tok, tok, tok, st,
                  pl.BlockSpec((1, D_MODEL), lambda b: (0, 0))],
        out_specs=[tok, st],
        out_shape=[jax.ShapeDtypeStruct((dec_batch, rows, D_MODEL), F32),
                   jax.ShapeDtypeStruct(state.shape, F32)],
        compiler_params=_cparams(("parallel",)),
        name="retention_sample",
    )(log_g, q, k, v, state, gn_g)


def _merge_kernel(x_ref, gt_ref, oa_ref, or_ref, gr_ref, ga_ref, gb_ref, w_ref, o_ref):
    gr = gr_ref[...].astype(F32)
    o_r = or_ref[...].astype(F32) * (gr * jax.nn.sigmoid(gr))
    merged = (jax.nn.sigmoid(ga_ref[...].astype(F32)) * oa_ref[...].astype(F32)
              + jax.nn.sigmoid(gb_ref[...].astype(F32)) * o_r)
    m = jnp.dot(merged.astype(BF16), w_ref[...], preferred_element_type=F32)
    o_ref[...] = x_ref[...] + gt_ref[...] * m


def _merge(x, mod, mod_kind, rows_per_mod, o_a, o_r, gr, ga, gb, w_out):
    rows = x.shape[0]
    tm = min(MERGE_TM, rows)
    rowspec = pl.BlockSpec((tm, D_MODEL), lambda i: (i, 0))
    return pl.pallas_call(
        _merge_kernel,
        grid=(rows // tm,),
        in_specs=[rowspec, _mod_spec(mod_kind, tm, rows_per_mod, 5), rowspec, rowspec, rowspec, rowspec, rowspec,
                  pl.BlockSpec((D_MODEL, D_MODEL), lambda i: (0, 0))],
        out_specs=rowspec,
        out_shape=jax.ShapeDtypeStruct((rows, D_MODEL), F32),
        compiler_params=_cparams(("parallel",)),
        name="merge_out_proj",
    )(x, mod, o_a, o_r, gr, ga, gb, w_out)


def kernel(x_prompt, x_sample, cache_k, cache_v, state_ret, page_table, c_prompt, c_sample, ada_w, ada_b, norm_ffn1, norm_mix, norm_ffn2, ffn1_w_in, ffn1_w_out, ffn2_w_in, ffn2_w_out, w_in, w_out, lam_q1, lam_k1, lam_q2, lam_k2, subln_g, ret_norm_g, norm_final):
    batch, seq, _ = x_prompt.shape
    dec_batch, dec_seq, _ = x_sample.shape
    depth = ada_w.shape[0]
    page = cache_k.shape[2]
    past_len = page_table.shape[1] * page
    log_g = jnp.log1p(-jnp.exp2(-5.0 - jnp.arange(RET_HEADS, dtype=F32)))
    nf = norm_final[None, :]

    tab_p = _rotation_tables(seq, 0)
    sub = 8
    tab_s = [jnp.tile(t[:dec_seq], (dec_batch, 1)) for t in _rotation_tables(sub, past_len)]

    n_c = batch + dec_batch
    c_all = jnp.concatenate([c_prompt, c_sample, jnp.zeros((-n_c % sub, D_MODEL), F32)], axis=0)

    yp = x_prompt.reshape(batch * seq, D_MODEL)
    ys = x_sample.reshape(dec_batch * dec_seq, D_MODEL)
    kp_l, vp_l, sp_l, ks_l, vs_l, ss_l = [], [], [], [], [], []
    for l in range(depth):
        lam_init = 0.8 - 0.6 * math.exp(-0.3 * l)
        mod_all = _adaln(c_all, ada_w[l], ada_b[l][None, :])
        mod_p = mod_all[:batch].reshape(batch, 1, N_MOD * D_MODEL)
        mod_s = jnp.repeat(mod_all[batch:n_c], dec_seq, axis=0)
        w1i, w1o = ffn1_w_in[l].astype(BF16), ffn1_w_out[l].astype(BF16)
        w2i, w2o = ffn2_w_in[l].astype(BF16), ffn2_w_out[l].astype(BF16)
        wi, wo = w_in[l].astype(BF16), w_out[l].astype(BF16)
        g1, gm, g2 = norm_ffn1[l][None, :], norm_mix[l][None, :], norm_ffn2[l][None, :]
        lam_vecs = (lam_q1[l][None, :], lam_k1[l][None, :], lam_q2[l][None, :], lam_k2[l][None, :])
        sg = subln_g[l][None, :]
        gn = ret_norm_g[l][None, :]
        last = l == depth - 1

        x1 = _ffn(yp, mod_p, "batch", seq, (0, 1, 2), g1, w1i, w1o, nf, False)
        (q, ktf, ktb, vf, vb, qr, krt, vr, gr, ga, gb) = _proj(
            x1, mod_p, "batch", seq, gm, wi, tab_p, seq // min(PROJ_TM, seq), True, batch, seq)
        o_a = _attn_prompt(q, ktb, vb, lam_vecs, sg, batch, seq, lam_init)
        o_r, sp = _ret_prompt(log_g, qr, krt, vr, gn, batch, seq)
        x2 = _merge(x1, mod_p, "batch", seq, o_a, o_r, gr, ga, gb, wo)
        yp = _ffn(x2, mod_p, "batch", seq, (6, 7, 8), g2, w2i, w2o, nf, last)
        kp_l.append(jnp.transpose(ktf.reshape(batch, 2 * DA_HEADS, DA_DK, seq), (0, 3, 1, 2)))
        vp_l.append(vf.reshape(batch, seq, DA_HEADS, DA_DV))
        sp_l.append(sp)

        rows_s = dec_batch * dec_seq
        x1 = _ffn(ys, mod_s, "token", 1, (0, 1, 2), g1, w1i, w1o, nf, False)
        (q, k, v, qr, kr, vr, gr, ga, gb) = _proj(
            x1, mod_s, "token", 1, gm, wi, tab_s, rows_s // min(PROJ_TM, rows_s), False, dec_batch, dec_seq)
        qh = q.reshape(dec_batch, dec_seq, DA_HEADS, 2, DA_DK).transpose(0, 2, 3, 1, 4)
        eye = jnp.eye(2, dtype=F32)
        q_bd = (qh[:, :, :, :, None, :] * eye[None, None, :, None, :, None]).reshape(
            dec_batch, DA_HEADS, 2 * dec_seq, DA_DV)
        to_heads = lambda a: a.reshape(dec_batch, dec_seq, DA_HEADS, DA_DV).transpose(0, 2, 1, 3)
        cache_kt = jnp.transpose(cache_k[l], (0, 2, 3, 1)).reshape(-1, DA_HEADS, DA_DV, page)
        o_a = _attn_sample(page_table, cache_kt, cache_v[l], q_bd, to_heads(k), to_heads(v), lam_vecs, sg, lam_init)
        pad = lambda a: jnp.pad(a.reshape(dec_batch, dec_seq, D_MODEL), ((0, 0), (0, sub - dec_seq), (0, 0)))
        o_r, ssm = _ret_sample(log_g, pad(qr), pad(kr), pad(vr), state_ret[l], gn, dec_seq)
        o_r = o_r[:, :dec_seq].reshape(rows_s, D_MODEL)
        x2 = _merge(x1, mod_s, "token", 1, o_a.reshape(rows_s, D_MODEL), o_r, gr, ga, gb, wo)
        ys = _ffn(x2, mod_s, "token", 1, (6, 7, 8), g2, w2i, w2o, nf, last)
        ks_l.append(k.reshape(dec_batch, dec_seq, 2 * DA_HEADS, DA_DK))
        vs_l.append(v.reshape(dec_batch, dec_seq, DA_HEADS, DA_DV))
        ss_l.append(ssm)

    return (yp.reshape(batch, seq, D_MODEL), ys.reshape(dec_batch, dec_seq, D_MODEL),
            jnp.stack(kp_l), jnp.stack(vp_l), jnp.stack(sp_l), jnp.stack(ks_l), jnp.stack(vs_l), jnp.stack(ss_l))
```

```python
import functools
import math

import jax
import jax.numpy as jnp
from jax import lax
from jax.experimental import pallas as pl
from jax.experimental.pallas import tpu as pltpu

F32 = jnp.float32
BF16 = jnp.bfloat16

D_MODEL = 1024
DA_DK = 64
DA_DV = 128
DA_HEADS = 8
RET_HEADS = 4
RET_DK = 256
RET_DV = 256
D_FF = 2816
N_MOD = 9
N_PROJ = 9
HALF_STEP = 0.5
ROPE_THETA = 10000.0
RET_THETA = 10000.0
NORM_EPS = 1e-6
SUBLN_EPS = 1e-5
GN_EPS = 1e-5
NEG_INF = -1e30
LOG2E = 1.4426950408889634
LANES = 128
VMEM_LIMIT = 56 * 1024 * 1024

FFN_TM = 512
FFN_TF = 1408
PROJ_TM = 256
ATTN_TQ = 512
ATTN_V_WIDTH = 2 * DA_HEADS * DA_DV
RET_L = 512
MERGE_TM = 512


def _cparams(sem):
    return pltpu.CompilerParams(dimension_semantics=sem, vmem_limit_bytes=VMEM_LIMIT)


def _rms(x, g, eps):
    return x * lax.rsqrt(jnp.mean(x * x, axis=-1, keepdims=True) + eps) * g


def _mod_spec(kind, tm, rows_per_mod, piece):
    if kind == "batch":
        return pl.BlockSpec((None, 1, D_MODEL), lambda i, *_: ((i * tm) // rows_per_mod, 0, piece))
    return pl.BlockSpec((tm, D_MODEL), lambda i, *_: (i, piece))


def _adaln_kernel(c_ref, w_ref, b_ref, o_ref):
    c = c_ref[...]
    a = (c * jax.nn.sigmoid(c)).astype(BF16)
    o_ref[...] = jnp.dot(a, w_ref[...].astype(BF16), preferred_element_type=F32) + b_ref[...]


def _adaln(c, w, b):
    m = c.shape[0]
    n = w.shape[1]
    tn = 1152
    return pl.pallas_call(
        _adaln_kernel,
        grid=(n // tn,),
        in_specs=[
            pl.BlockSpec((m, D_MODEL), lambda j: (0, 0)),
            pl.BlockSpec((D_MODEL, tn), lambda j: (0, j)),
            pl.BlockSpec((1, tn), lambda j: (0, j)),
        ],
        out_specs=pl.BlockSpec((m, tn), lambda j: (0, j)),
        out_shape=jax.ShapeDtypeStruct((m, n), F32),
        compiler_params=_cparams(("arbitrary",)),
        name="adaln",
    )(c, w, b)


def _ffn_kernel(x_ref, sh_ref, sc_ref, gt_ref, g_ref, wa_ref, wb_ref, wo_ref, nf_ref, o_ref, h_ref, acc_ref,
                *, n_ff, final_norm):
    j = pl.program_id(1)

    @pl.when(j == 0)
    def _():
        h = _rms(x_ref[...], g_ref[...], NORM_EPS) * (1.0 + sc_ref[...]) + sh_ref[...]
        h_ref[...] = h.astype(BF16)

    h = h_ref[...]
    a = jnp.dot(h, wa_ref[...], preferred_element_type=F32)
    b = jnp.dot(h, wb_ref[...], preferred_element_type=F32)
    act = (a * jax.nn.sigmoid(a) * b).astype(BF16)
    part = jnp.dot(act, wo_ref[...], preferred_element_type=F32)

    @pl.when(j == 0)
    def _():
        acc_ref[...] = part

    @pl.when(j > 0)
    def _():
        acc_ref[...] += part

    @pl.when(j == n_ff - 1)
    def _():
        out = x_ref[...] + HALF_STEP * gt_ref[...] * acc_ref[...]
        if final_norm:
            out = _rms(out, nf_ref[...], NORM_EPS)
        o_ref[...] = out


def _ffn(x, mod, mod_kind, rows_per_mod, pieces, norm_g, w_in, w_out, norm_final, final_norm):
    rows = x.shape[0]
    tm = min(FFN_TM, rows)
    tf = FFN_TF
    n_ff = D_FF // tf
    row = lambda i, j: (i, 0)
    const = lambda i, j: (0, 0)
    return pl.pallas_call(
        functools.partial(_ffn_kernel, n_ff=n_ff, final_norm=final_norm),
        grid=(rows // tm, n_ff),
        in_specs=[
            pl.BlockSpec((tm, D_MODEL), row),
            _mod_spec(mod_kind, tm, rows_per_mod, pieces[0]),
            _mod_spec(mod_kind, tm, rows_per_mod, pieces[1]),
            _mod_spec(mod_kind, tm, rows_per_mod, pieces[2]),
            pl.BlockSpec((1, D_MODEL), const),
            pl.BlockSpec((D_MODEL, tf), lambda i, j: (0, j)),
            pl.BlockSpec((D_MODEL, tf), lambda i, j: (0, n_ff + j)),
            pl.BlockSpec((tf, D_MODEL), lambda i, j: (j, 0)),
            pl.BlockSpec((1, D_MODEL), const),
        ],
        out_specs=pl.BlockSpec((tm, D_MODEL), row),
        out_shape=jax.ShapeDtypeStruct((rows, D_MODEL), F32),
        scratch_shapes=[pltpu.VMEM((tm, D_MODEL), BF16), pltpu.VMEM((tm, D_MODEL), F32)],
        compiler_params=_cparams(("parallel", "arbitrary")),
        name="ffn",
    )(x, mod, mod, mod, norm_g, w_in, w_in, w_out, norm_final)


def _table_kernel(inv_r_ref, sgn_r_ref, inv_t_ref, sgn_t_ref, cr_ref, sr_ref, ct_ref, st_ref, *, offset):
    rows = cr_ref.shape[0]
    pos = (lax.broadcasted_iota(jnp.int32, (rows, 1), 0) + (pl.program_id(0) * rows + offset)).astype(F32)
    ang = pos * inv_r_ref[...]
    cr_ref[...] = jnp.cos(ang)
    sr_ref[...] = jnp.sin(ang) * sgn_r_ref[...]
    ang = pos * inv_t_ref[...]
    ct_ref[...] = jnp.cos(ang)
    st_ref[...] = jnp.sin(ang) * sgn_t_ref[...]


def _rotation_tables(n_pos, offset):
    lane = jnp.arange(LANES)
    half = DA_DK // 2
    inv_r = (ROPE_THETA ** (-((lane % DA_DK) % half).astype(F32) / half))[None, :]
    sgn_r = jnp.where((lane % DA_DK) < half, -1.0, 1.0).astype(F32)[None, :]
    lane_t = jnp.arange(RET_DK)
    angle = 1.0 / (RET_THETA ** jnp.linspace(0.0, 1.0, RET_DK // 2, dtype=F32))
    inv_t = angle[lane_t // 2][None, :]
    sgn_t = jnp.where(lane_t % 2 == 0, -1.0, 1.0).astype(F32)[None, :]
    rows = min(n_pos, 512)
    vec = lambda w: pl.BlockSpec((1, w), lambda i: (0, 0))
    tab = lambda w: pl.BlockSpec((rows, w), lambda i: (i, 0))
    return pl.pallas_call(
        functools.partial(_table_kernel, offset=offset),
        grid=(n_pos // rows,),
        in_specs=[vec(LANES), vec(LANES), vec(RET_DK), vec(RET_DK)],
        out_specs=[tab(LANES), tab(LANES), tab(RET_DK), tab(RET_DK)],
        out_shape=[jax.ShapeDtypeStruct((n_pos, LANES), F32), jax.ShapeDtypeStruct((n_pos, LANES), F32),
                   jax.ShapeDtypeStruct((n_pos, RET_DK), F32), jax.ShapeDtypeStruct((n_pos, RET_DK), F32)],
        compiler_params=_cparams(("arbitrary",)),
        name="rotation_tables",
    )(inv_r, sgn_r, inv_t, sgn_t)


def _pair_rotate(x, cos, sin_signed, shift):
    lane = lax.broadcasted_iota(jnp.int32, x.shape, 1)
    partner = jnp.where((lane % (2 * shift)) < shift, pltpu.roll(x, LANES - shift, 1), pltpu.roll(x, shift, 1))
    return x * cos + partner * sin_signed


def _proj_kernel(x_ref, sh_ref, sc_ref, g_ref, w_ref, cr_ref, sr_ref, ct_ref, st_ref, *out_refs, transposed):
    h = (_rms(x_ref[...], g_ref[...], NORM_EPS) * (1.0 + sc_ref[...]) + sh_ref[...]).astype(BF16)
    cr, sr = cr_ref[...], sr_ref[...]
    n_chunks = D_MODEL // LANES
    q_scale = (DA_DK ** -0.5) * LOG2E
    k_scale = RET_DK ** -0.5

    def group(gi):
        return jnp.dot(h, w_ref[:, gi * D_MODEL:(gi + 1) * D_MODEL], preferred_element_type=F32)

    def chunk(p, c):
        return p[:, c * LANES:(c + 1) * LANES]

    def ret_tab(t_ref, c):
        half = (c % (RET_DK // LANES)) * LANES
        return t_ref[:, half:half + LANES]

    if transposed:
        (q_ref, ktf_ref, ktb_ref, vf_ref, vb_ref, qr_ref, krt_ref, vr_ref, gr_ref, ga_ref, gb_ref) = out_refs
    else:
        (q_ref, k_ref, v_ref, qr_ref, kr_ref, vr_ref, gr_ref, ga_ref, gb_ref) = out_refs

    p = group(0)
    for c in range(n_chunks):
        q_ref[:, c * LANES:(c + 1) * LANES] = (_pair_rotate(chunk(p, c), cr, sr, DA_DK // 2) * q_scale).astype(q_ref.dtype)
    p = group(1)
    for c in range(n_chunks):
        kc = _pair_rotate(chunk(p, c), cr, sr, DA_DK // 2)
        if transposed:
            kt = kc.T
            ktf_ref[c * LANES:(c + 1) * LANES, :] = kt
            ktb_ref[c * LANES:(c + 1) * LANES, :] = kt.astype(BF16)
        else:
            k_ref[:, c * LANES:(c + 1) * LANES] = kc
    p = group(2)
    if transposed:
        vf_ref[...] = p
        ones = jnp.ones((p.shape[0], DA_DV), BF16)
        for hd in range(DA_HEADS):
            vb_ref[:, 2 * hd * DA_DV:(2 * hd + 1) * DA_DV] = chunk(p, hd).astype(BF16)
            vb_ref[:, (2 * hd + 1) * DA_DV:(2 * hd + 2) * DA_DV] = ones
    else:
        v_ref[...] = p
    p = group(3)
    for c in range(n_chunks):
        qc = _pair_rotate(chunk(p, c), ret_tab(ct_ref, c), ret_tab(st_ref, c), 1)
        qr_ref[:, c * LANES:(c + 1) * LANES] = qc.astype(qr_ref.dtype)
    p = group(4)
    for c in range(n_chunks):
        kc = _pair_rotate(chunk(p, c), ret_tab(ct_ref, c), ret_tab(st_ref, c), 1) * k_scale
        if transposed:
            krt_ref[c * LANES:(c + 1) * LANES, :] = kc.T.astype(BF16)
        else:
            kr_ref[:, c * LANES:(c + 1) * LANES] = kc
    for gi, ref in ((5, vr_ref), (6, gr_ref), (7, ga_ref), (8, gb_ref)):
        ref[...] = group(gi).astype(ref.dtype)


def _proj(x, mod, mod_kind, rows_per_mod, norm_g, w_in, tables, n_tab_tiles, transposed, batch, seq):
    rows = x.shape[0]
    tm = min(PROJ_TM, rows)
    row = lambda i: (i, 0)
    const = lambda i: (0, 0)
    tab = lambda w: pl.BlockSpec((tm, w), lambda i: (i % n_tab_tiles, 0))
    rowspec = pl.BlockSpec((tm, D_MODEL), row)
    if transposed:
        tiles = seq // tm
        tspec = pl.BlockSpec((None, D_MODEL, tm), lambda i: (i // tiles, 0, i % tiles))
        t_shape = lambda dt: jax.ShapeDtypeStruct((batch, D_MODEL, seq), dt)
        r_shape = lambda dt: jax.ShapeDtypeStruct((rows, D_MODEL), dt)
        vspec = pl.BlockSpec((tm, ATTN_V_WIDTH), row)
        v_shape = jax.ShapeDtypeStruct((rows, ATTN_V_WIDTH), BF16)
        out_specs = [rowspec, tspec, tspec, rowspec, vspec, rowspec, tspec, rowspec, rowspec, rowspec, rowspec]
        out_shape = [r_shape(BF16), t_shape(F32), t_shape(BF16), r_shape(F32), v_shape, r_shape(BF16),
                     t_shape(BF16), r_shape(BF16), r_shape(BF16), r_shape(BF16), r_shape(BF16)]
    else:
        out_specs = [rowspec] * N_PROJ
        out_shape = [jax.ShapeDtypeStruct((rows, D_MODEL), F32)] * N_PROJ
    return pl.pallas_call(
        functools.partial(_proj_kernel, transposed=transposed),
        grid=(rows // tm,),
        in_specs=[
            rowspec,
            _mod_spec(mod_kind, tm, rows_per_mod, 3),
            _mod_spec(mod_kind, tm, rows_per_mod, 4),
            pl.BlockSpec((1, D_MODEL), const),
            pl.BlockSpec((D_MODEL, N_PROJ * D_MODEL), const, pipeline_mode=pl.Buffered(1)),
            tab(LANES), tab(LANES), tab(RET_DK), tab(RET_DK),
        ],
        out_specs=out_specs,
        out_shape=out_shape,
        compiler_params=_cparams(("parallel",)),
        name="mixer_proj",
    )(x, mod, mod, norm_g, w_in, *tables)


def _lambda(lq1, lk1, lq2, lk2, lam_init):
    return (jnp.exp(jnp.sum(lq1 * lk1, axis=-1, keepdims=True))
            - jnp.exp(jnp.sum(lq2 * lk2, axis=-1, keepdims=True)) + lam_init)


def _attn_kernel(qi_ref, kj_ref, q_ref, kt_ref, v_ref, lq1_ref, lk1_ref, lq2_ref, lk2_ref, sg_ref, o_ref,
                 qz_ref, m_ref, acc_ref, *, tq, lam_init):
    step = pl.program_id(1)
    qi = qi_ref[step]
    kj = kj_ref[step]
    lane = lax.broadcasted_iota(jnp.int32, (tq, DA_DV), 1)
    n_rep = tq // LANES

    @pl.when(kj == 0)
    def _():
        for h in range(DA_HEADS):
            qh = q_ref[:, h * DA_DV:(h + 1) * DA_DV]
            qz_ref[h, :tq, :] = jnp.where(lane < DA_DK, qh, jnp.zeros_like(qh))
            qz_ref[h, tq:, :] = jnp.where(lane >= DA_DK, qh, jnp.zeros_like(qh))
        m_ref[...] = jnp.full(m_ref.shape, NEG_INF, F32)
        acc_ref[...] = jnp.zeros(acc_ref.shape, F32)

    def block(masked):
        if masked:
            r = lax.broadcasted_iota(jnp.int32, (2 * tq, tq), 0) % tq
            c = lax.broadcasted_iota(jnp.int32, (2 * tq, tq), 1)
            keep = c <= r
        for h in range(DA_HEADS):
            s = jnp.dot(qz_ref[h], kt_ref[h * DA_DV:(h + 1) * DA_DV, :], preferred_element_type=F32)
            if masked:
                s = jnp.where(keep, s, NEG_INF)
            m_old = m_ref[h]
            m_new = jnp.maximum(m_old, jnp.max(s, axis=-1, keepdims=True))
            alpha = jnp.exp2(m_old - m_new)
            p = jnp.exp2(s - jnp.concatenate([m_new] * n_rep, axis=1))
            pv = jnp.dot(p.astype(BF16), v_ref[:, h * 2 * DA_DV:(h + 1) * 2 * DA_DV], preferred_element_type=F32)
            acc_ref[h] = jnp.concatenate([alpha, alpha], axis=1) * acc_ref[h] + pv
            m_ref[h] = m_new

    @pl.when(kj < qi)
    def _():
        block(False)

    @pl.when(kj == qi)
    def _():
        block(True)
        lam = _lambda(lq1_ref[...], lk1_ref[...], lq2_ref[...], lk2_ref[...], lam_init)
        for h in range(DA_HEADS):
            o = acc_ref[h, :, :DA_DV] / acc_ref[h, :, DA_DV:]
            d = o[:tq] - lam * o[tq:]
            o_ref[:, h * DA_DV:(h + 1) * DA_DV] = (_rms(d, sg_ref[...], SUBLN_EPS) * (1.0 - lam_init)).astype(o_ref.dtype)


def _attn_prompt(q, kt, v, lam_vecs, subln_g, batch, seq, lam_init):
    tq = min(ATTN_TQ, seq)
    nq = seq // tq
    pairs = [(i, j) for i in range(nq) for j in range(i + 1)]
    qi = jnp.asarray([p[0] for p in pairs], jnp.int32)
    kj = jnp.asarray([p[1] for p in pairs], jnp.int32)
    vec = lambda w: pl.BlockSpec((1, w), lambda b, s, qi, kj: (0, 0))
    grid_spec = pltpu.PrefetchScalarGridSpec(
        num_scalar_prefetch=2,
        grid=(batch, len(pairs)),
        in_specs=[
            pl.BlockSpec((tq, D_MODEL), lambda b, s, qi, kj: (b * nq + qi[s], 0)),
            pl.BlockSpec((None, D_MODEL, tq), lambda b, s, qi, kj: (b, 0, kj[s])),
            pl.BlockSpec((tq, ATTN_V_WIDTH), lambda b, s, qi, kj: (b * nq + kj[s], 0)),
            vec(DA_DK), vec(DA_DK), vec(DA_DK), vec(DA_DK), vec(DA_DV),
        ],
        out_specs=pl.BlockSpec((tq, D_MODEL), lambda b, s, qi, kj: (b * nq + qi[s], 0)),
        scratch_shapes=[
            pltpu.VMEM((DA_HEADS, 2 * tq, DA_DV), BF16),
            pltpu.VMEM((DA_HEADS, 2 * tq, LANES), F32),
            pltpu.VMEM((DA_HEADS, 2 * tq, 2 * DA_DV), F32),
        ],
    )
    return pl.pallas_call(
        functools.partial(_attn_kernel, tq=tq, lam_init=lam_init),
        grid_spec=grid_spec,
        out_shape=jax.ShapeDtypeStruct((batch * seq, D_MODEL), BF16),
        compiler_params=_cparams(("parallel", "arbitrary")),
        name="diff_attn_prompt",
    )(qi, kj, q, kt, v, *lam_vecs, subln_g)


def _attn_sample_kernel(pt_ref, *refs, n_pages, page, dec_seq, lam_init):
    k_refs = refs[:n_pages]
    v_refs = refs[n_pages:2 * n_pages]
    (q_ref, kn_ref, vn_ref, lq1_ref, lk1_ref, lq2_ref, lk2_ref, sg_ref, o_ref, s_ref) = refs[2 * n_pages:]
    del pt_ref
    rows = 2 * dec_seq
    q = q_ref[...].astype(BF16)
    for i in range(n_pages):
        s_ref[:, :, i * page:(i + 1) * page] = jnp.einsum(
            "hrk,hkt->hrt", q, k_refs[i][...].astype(BF16), preferred_element_type=F32)
    qf = q.astype(F32)
    kn = kn_ref[...].astype(BF16).astype(F32)
    vn = vn_ref[...].astype(BF16).astype(F32)
    r = lax.broadcasted_iota(jnp.int32, (DA_HEADS, rows, 1), 1) % dec_seq
    s_new = [jnp.where(r >= t, jnp.sum(qf * kn[:, t:t + 1, :], axis=-1, keepdims=True), NEG_INF)
             for t in range(dec_seq)]
    s_old = s_ref[...]
    m = jnp.max(s_old, axis=-1, keepdims=True)
    for s_t in s_new:
        m = jnp.maximum(m, s_t)
    p_old = jnp.exp2(s_old - m)
    denom = jnp.sum(p_old, axis=-1, keepdims=True)
    acc_new = jnp.zeros((DA_HEADS, rows, DA_DV), F32)
    for t, s_t in enumerate(s_new):
        p_t = jnp.exp2(s_t - m)
        denom = denom + p_t
        acc_new = acc_new + p_t.astype(BF16).astype(F32) * vn[:, t:t + 1, :]
    s_ref[...] = p_old
    lam = _lambda(lq1_ref[...], lk1_ref[...], lq2_ref[...], lk2_ref[...], lam_init)
    for h in range(DA_HEADS):
        acc = acc_new[h]
        for i in range(n_pages):
            acc = acc + jnp.dot(s_ref[h, :, i * page:(i + 1) * page].astype(BF16),
                                v_refs[i][pl.ds(h, page, stride=DA_HEADS), :].astype(BF16),
                                preferred_element_type=F32)
        o = acc / denom[h]
        d = o[:dec_seq] - lam * o[dec_seq:]
        o_ref[:, h * DA_DV:(h + 1) * DA_DV] = _rms(d, sg_ref[...], SUBLN_EPS) * (1.0 - lam_init)


def _attn_sample(page_table, cache_kt, cache_v, q_bd, k_new, v_new, lam_vecs, subln_g, lam_init):
    dec_batch, n_pages = page_table.shape
    page = cache_v.shape[1]
    dec_seq = k_new.shape[2]
    kspec = lambda i: pl.BlockSpec((None, DA_HEADS, DA_DV, page), lambda b, pt: (pt[b, i], 0, 0, 0))
    cache_v = cache_v.reshape(cache_v.shape[0], page * DA_HEADS, DA_DV)
    vspec = lambda i: pl.BlockSpec((None, page * DA_HEADS, DA_DV), lambda b, pt: (pt[b, i], 0, 0))
    per_b = lambda r: pl.BlockSpec((None, DA_HEADS, r, DA_DV), lambda b, pt: (b, 0, 0, 0))
    vec = lambda w: pl.BlockSpec((1, w), lambda b, pt: (0, 0))
    grid_spec = pltpu.PrefetchScalarGridSpec(
        num_scalar_prefetch=1,
        grid=(dec_batch,),
        in_specs=[kspec(i) for i in range(n_pages)] + [vspec(i) for i in range(n_pages)] + [
            per_b(2 * dec_seq), per_b(dec_seq), per_b(dec_seq),
            vec(DA_DK), vec(DA_DK), vec(DA_DK), vec(DA_DK), vec(DA_DV)],
        out_specs=pl.BlockSpec((None, dec_seq, D_MODEL), lambda b, pt: (b, 0, 0)),
        scratch_shapes=[pltpu.VMEM((DA_HEADS, 2 * dec_seq, n_pages * page), F32)],
    )
    return pl.pallas_call(
        functools.partial(_attn_sample_kernel, n_pages=n_pages, page=page, dec_seq=dec_seq, lam_init=lam_init),
        grid_spec=grid_spec,
        out_shape=jax.ShapeDtypeStruct((dec_batch, dec_seq, D_MODEL), F32),
        compiler_params=_cparams(("parallel",)),
        name="diff_attn_sample",
    )(page_table, *([cache_kt] * n_pages), *([cache_v] * n_pages), q_bd, k_new, v_new, *lam_vecs, subln_g)


def _group_norm(o, g):
    mu = jnp.mean(o, axis=-1, keepdims=True)
    d = o - mu
    return d * lax.rsqrt(jnp.mean(d * d, axis=-1, keepdims=True) + GN_EPS) * g


def _ret_prompt_kernel(lg_ref, q_ref, kt_ref, v_ref, g_ref, o_ref, st_ref, state_ref, decay_ref, *, chunk):
    h = pl.program_id(1)
    c = pl.program_id(2)
    lg = lg_ref[h]

    @pl.when(c == 0)
    def _():
        state_ref[...] = jnp.zeros(state_ref.shape, F32)
        ri = lax.broadcasted_iota(jnp.int32, (chunk, chunk), 0)
        ci = lax.broadcasted_iota(jnp.int32, (chunk, chunk), 1)
        dist = (ri - ci).astype(F32)
        decay_ref[...] = jnp.where(dist >= 0, jnp.exp(jnp.maximum(dist, 0.0) * lg), 0.0)

    q = q_ref[...]
    kt = kt_ref[...]
    v = v_ref[...]
    scores = jnp.dot(q, kt, preferred_element_type=F32) * decay_ref[...]
    inner = jnp.dot(scores.astype(BF16), v, preferred_element_type=F32)
    q_decay = jnp.exp((lax.broadcasted_iota(jnp.int32, (chunk, 1), 0) + 1).astype(F32) * lg)
    state = state_ref[...]
    cross = jnp.dot(q, state.astype(BF16), preferred_element_type=F32) * q_decay
    o_ref[...] = _group_norm(inner + cross, g_ref[...]).astype(o_ref.dtype)
    k_decay = jnp.exp((chunk - 1 - lax.broadcasted_iota(jnp.int32, (1, chunk), 1)).astype(F32) * lg)
    ktd = (kt.astype(F32) * k_decay).astype(BF16)
    new_state = jnp.exp(jnp.full((1, 1), chunk, F32) * lg) * state + jnp.dot(ktd, v, preferred_element_type=F32)
    state_ref[...] = new_state

    @pl.when(c == pl.num_programs(2) - 1)
    def _():
        st_ref[...] = new_state


def _ret_prompt(log_g, q, kt, v, gn_g, batch, seq):
    chunk = min(RET_L, seq)
    nc = seq // chunk
    rowspec = pl.BlockSpec((chunk, RET_DV), lambda b, h, c: (b * nc + c, h))
    return pl.pallas_call(
        functools.partial(_ret_prompt_kernel, chunk=chunk),
        grid=(batch, RET_HEADS, nc),
        in_specs=[
            pl.BlockSpec(memory_space=pltpu.SMEM),
            rowspec,
            pl.BlockSpec((None, RET_DK, chunk), lambda b, h, c: (b, h, c)),
            rowspec,
            pl.BlockSpec((1, RET_DV), lambda b, h, c: (0, h)),
        ],
        out_specs=[rowspec, pl.BlockSpec((None, None, RET_DK, RET_DV), lambda b, h, c: (b, h, 0, 0))],
        out_shape=[jax.ShapeDtypeStruct((batch * seq, D_MODEL), BF16),
                   jax.ShapeDtypeStruct((batch, RET_HEADS, RET_DK, RET_DV), F32)],
        scratch_shapes=[pltpu.VMEM((RET_DK, RET_DV), F32), pltpu.VMEM((chunk, chunk), F32)],
        compiler_params=_cparams(("parallel", "parallel", "arbitrary")),
        name="retention_prompt",
    )(log_g, q, kt, v, gn_g)


def _ret_sample_kernel(lg_ref, q_ref, k_ref, v_ref, s_ref, g_ref, o_ref, st_ref, *, dec_seq):
    rows = q_ref.shape[0]
    ri = lax.broadcasted_iota(jnp.int32, (rows, 1), 0)
    for h in range(RET_HEADS):
        lg = lg_ref[h]
        sl = slice(h * RET_DK, (h + 1) * RET_DK)
        q = q_ref[:, sl]
        k = k_ref[:, sl]
        v = v_ref[:, sl]
        qb = q.astype(BF16).astype(F32)
        kb = k.astype(BF16).astype(F32)
        vb = v.astype(BF16).astype(F32)
        state = s_ref[h]
        inner = jnp.zeros((rows, RET_DV), F32)
        for j in range(dec_seq):
            dist = (ri - j).astype(F32)
            decay = jnp.where(dist >= 0, jnp.exp(jnp.maximum(dist, 0.0) * lg), 0.0)
            score = jnp.sum(qb * kb[j:j + 1, :], axis=-1, keepdims=True) * decay
            inner = inner + score.astype(BF16).astype(F32) * vb[j:j + 1, :]
        q_decay = jnp.exp((ri + 1).astype(F32) * lg)
        cross = jnp.dot(q.astype(BF16), state.astype(BF16), preferred_element_type=F32) * q_decay
        o_ref[:, sl] = _group_norm(inner + cross, g_ref[:, sl])
        k_decay = jnp.where(ri < dec_seq, jnp.exp((dec_seq - 1 - ri).astype(F32) * lg), 0.0)
        zpad = jnp.zeros((LANES - rows, RET_DK), F32)
        kd_t = jnp.concatenate([k * k_decay, zpad], axis=0).T.astype(BF16)
        v_pad = jnp.concatenate([v, zpad], axis=0).astype(BF16)
        upd = jnp.dot(kd_t, v_pad, preferred_element_type=F32)
        st_ref[h] = jnp.exp(jnp.full((1, 1), dec_seq, F32) * lg) * state + upd


def _ret_sample(log_g, q, k, v, state, gn_g, dec_seq):
    dec_batch, rows = q.shape[:2]
    tok = pl.BlockSpec((None, rows, D_MODEL), lambda b: (b, 0, 0))
    st = pl.BlockSpec((None, RET_HEADS, RET_DK, RET_DV), lambda b: (b, 0, 0, 0))
    return pl.pallas_call(
        functools.partial(_ret_sample_kernel, dec_seq=dec_seq),
        grid=(dec_batch,),
        in_specs=[pl.BlockSpec(memory_space=pltpu.SMEM), tok, tok, tok, st,
                  pl.BlockSpec((1, D_MODEL), lambda b: (0, 0))],
        out_specs=[tok, st],
        out_shape=[jax.ShapeDtypeStruct((dec_batch, rows, D_MODEL), F32),
                   jax.ShapeDtypeStruct(state.shape, F32)],
        compiler_params=_cparams(("parallel",)),
        name="retention_sample",
    )(log_g, q, k, v, state, gn_g)


def _merge_kernel(x_ref, gt_ref, oa_ref, or_ref, gr_ref, ga_ref, gb_ref, w_ref, o_ref):
    gr = gr_ref[...].astype(F32)
    o_r = or_ref[...].astype(F32) * (gr * jax.nn.sigmoid(gr))
    merged = (jax.nn.sigmoid(ga_ref[...].astype(F32)) * oa_ref[...].astype(F32)
              + jax.nn.sigmoid(gb_ref[...].astype(F32)) * o_r)
    m = jnp.dot(merged.astype(BF16), w_ref[...], preferred_element_type=F32)
    o_ref[...] = x_ref[...] + gt_ref[...] * m


def _merge(x, mod, mod_kind, rows_per_mod, o_a, o_r, gr, ga, gb, w_out):
    rows = x.shape[0]
    tm = min(MERGE_TM, rows)
    rowspec = pl.BlockSpec((tm, D_MODEL), lambda i: (i, 0))
    return pl.pallas_call(
        _merge_kernel,
        grid=(rows // tm,),
        in_specs=[rowspec, _mod_spec(mod_kind, tm, rows_per_mod, 5), rowspec, rowspec, rowspec, rowspec, rowspec,
                  pl.BlockSpec((D_MODEL, D_MODEL), lambda i: (0, 0))],
        out_specs=rowspec,
        out_shape=jax.ShapeDtypeStruct((rows, D_MODEL), F32),
        compiler_params=_cparams(("parallel",)),
        name="merge_out_proj",
    )(x, mod, o_a, o_r, gr, ga, gb, w_out)


def kernel(x_prompt, x_sample, cache_k, cache_v, state_ret, page_table, c_prompt, c_sample, ada_w, ada_b, norm_ffn1, norm_mix, norm_ffn2, ffn1_w_in, ffn1_w_out, ffn2_w_in, ffn2_w_out, w_in, w_out, lam_q1, lam_k1, lam_q2, lam_k2, subln_g, ret_norm_g, norm_final):
    batch, seq, _ = x_prompt.shape
    dec_batch, dec_seq, _ = x_sample.shape
    depth = ada_w.shape[0]
    page = cache_k.shape[2]
    past_len = page_table.shape[1] * page
    log_g = jnp.log1p(-jnp.exp2(-5.0 - jnp.arange(RET_HEADS, dtype=F32)))
    nf = norm_final[None, :]

    tab_p = _rotation_tables(seq, 0)
    sub = 8
    tab_s = [jnp.tile(t[:dec_seq], (dec_batch, 1)) for t in _rotation_tables(sub, past_len)]

    n_c = batch + dec_batch
    c_all = jnp.concatenate([c_prompt, c_sample, jnp.zeros((-n_c % sub, D_MODEL), F32)], axis=0)

    yp = x_prompt.reshape(batch * seq, D_MODEL)
    ys = x_sample.reshape(dec_batch * dec_seq, D_MODEL)
    kp_l, vp_l, sp_l, ks_l, vs_l, ss_l = [], [], [], [], [], []
    for l in range(depth):
        lam_init = 0.8 - 0.6 * math.exp(-0.3 * l)
        mod_all = _adaln(c_all, ada_w[l], ada_b[l][None, :])
        mod_p = mod_all[:batch].reshape(batch, 1, N_MOD * D_MODEL)
        mod_s = jnp.repeat(mod_all[batch:n_c], dec_seq, axis=0)
        w1i, w1o = ffn1_w_in[l].astype(BF16), ffn1_w_out[l].astype(BF16)
        w2i, w2o = ffn2_w_in[l].astype(BF16), ffn2_w_out[l].astype(BF16)
        wi, wo = w_in[l].astype(BF16), w_out[l].astype(BF16)
        g1, gm, g2 = norm_ffn1[l][None, :], norm_mix[l][None, :], norm_ffn2[l][None, :]
        lam_vecs = (lam_q1[l][None, :], lam_k1[l][None, :], lam_q2[l][None, :], lam_k2[l][None, :])
        sg = subln_g[l][None, :]
        gn = ret_norm_g[l][None, :]
        last = l == depth - 1

        x1 = _ffn(yp, mod_p, "batch", seq, (0, 1, 2), g1, w1i, w1o, nf, False)
        (q, ktf, ktb, vf, vb, qr, krt, vr, gr, ga, gb) = _proj(
            x1, mod_p, "batch", seq, gm, wi, tab_p, seq // min(PROJ_TM, seq), True, batch, seq)
        o_a = _attn_prompt(q, ktb, vb, lam_vecs, sg, batch, seq, lam_init)
        o_r, sp = _ret_prompt(log_g, qr, krt, vr, gn, batch, seq)
        x2 = _merge(x1, mod_p, "batch", seq, o_a, o_r, gr, ga, gb, wo)
        yp = _ffn(x2, mod_p, "batch", seq, (6, 7, 8), g2, w2i, w2o, nf, last)
        kp_l.append(jnp.transpose(ktf.reshape(batch, 2 * DA_HEADS, DA_DK, seq), (0, 3, 1, 2)))
        vp_l.append(vf.reshape(batch, seq, DA_HEADS, DA_DV))
        sp_l.append(sp)

        rows_s = dec_batch * dec_seq
        x1 = _ffn(ys, mod_s, "token", 1, (0, 1, 2), g1, w1i, w1o, nf, False)
        (q, k, v, qr, kr, vr, gr, ga, gb) = _proj(
            x1, mod_s, "token", 1, gm, wi, tab_s, rows_s // min(PROJ_TM, rows_s), False, dec_batch, dec_seq)
        qh = q.reshape(dec_batch, dec_seq, DA_HEADS, 2, DA_DK).transpose(0, 2, 3, 1, 4)
        eye = jnp.eye(2, dtype=F32)
        q_bd = (qh[:, :, :, :, None, :] * eye[None, None, :, None, :, None]).reshape(
            dec_batch, DA_HEADS, 2 * dec_seq, DA_DV)
        to_heads = lambda a: a.reshape(dec_batch, dec_seq, DA_HEADS, DA_DV).transpose(0, 2, 1, 3)
        cache_kt = jnp.transpose(cache_k[l], (0, 2, 3, 1)).reshape(-1, DA_HEADS, DA_DV, page)
        o_a = _attn_sample(page_table, cache_kt, cache_v[l], q_bd, to_heads(k), to_heads(v), lam_vecs, sg, lam_init)
        pad = lambda a: jnp.pad(a.reshape(dec_batch, dec_seq, D_MODEL), ((0, 0), (0, sub - dec_seq), (0, 0)))
        o_r, ssm = _ret_sample(log_g, pad(qr), pad(kr), pad(vr), state_ret[l], gn, dec_seq)
        o_r = o_r[:, :dec_seq].reshape(rows_s, D_MODEL)
        x2 = _merge(x1, mod_s, "token", 1, o_a.reshape(rows_s, D_MODEL), o_r, gr, ga, gb, wo)
        ys = _ffn(x2, mod_s, "token", 1, (6, 7, 8), g2, w2i, w2o, nf, last)
        ks_l.append(k.reshape(dec_batch, dec_seq, 2 * DA_HEADS, DA_DK))
        vs_l.append(v.reshape(dec_batch, dec_seq, DA_HEADS, DA_DV))
        ss_l.append(ssm)

    return (yp.reshape(batch, seq, D_MODEL), ys.reshape(dec_batch, dec_seq, D_MODEL),
            jnp.stack(kp_l), jnp.stack(vp_l), jnp.stack(sp_l), jnp.stack(ks_l), jnp.stack(vs_l), jnp.stack(ss_l))
```

```python
import functools
import math

import jax
import jax.numpy as jnp
from jax import lax
from jax.experimental import pallas as pl
from jax.experimental.pallas import tpu as pltpu

F32 = jnp.float32
BF16 = jnp.bfloat16

D_MODEL = 1024
DA_DK = 64
DA_DV = 128
DA_HEADS = 8
RET_HEADS = 4
RET_DK = 256
RET_DV = 256
D_FF = 2816
N_MOD = 9
N_PROJ = 9
HALF_STEP = 0.5
ROPE_THETA = 10000.0
RET_THETA = 10000.0
NORM_EPS = 1e-6
SUBLN_EPS = 1e-5
GN_EPS = 1e-5
NEG_INF = -1e30
LOG2E = 1.4426950408889634
LANES = 128
VMEM_LIMIT = 56 * 1024 * 1024

FFN_TM = 512
FFN_TF = 1408
PROJ_TM = 256
ATTN_TQ = 512
ATTN_V_WIDTH = 2 * DA_HEADS * DA_DV
RET_L = 512
RET_SAMPLE_NB = 4


def _cparams(sem):
    return pltpu.CompilerParams(dimension_semantics=sem, vmem_limit_bytes=VMEM_LIMIT)


def _rms(x, g, eps):
    return x * lax.rsqrt(jnp.mean(x * x, axis=-1, keepdims=True) + eps) * g


def _mod_spec(kind, tm, rows_per_mod, piece):
    if kind == "batch":
        return pl.BlockSpec((None, 1, D_MODEL), lambda i, *_: ((i * tm) // rows_per_mod, 0, piece))
    return pl.BlockSpec((tm, D_MODEL), lambda i, *_: (i, piece))


def _adaln_kernel(c_ref, w_ref, b_ref, o_ref):
    c = c_ref[...]
    a = (c * jax.nn.sigmoid(c)).astype(BF16)
    o_ref[...] = jnp.dot(a, w_ref[...].astype(BF16), preferred_element_type=F32) + b_ref[...]


def _adaln(c, w, b):
    m = c.shape[0]
    n = w.shape[1]
    tn = 1152
    return pl.pallas_call(
        _adaln_kernel,
        grid=(n // tn,),
        in_specs=[
            pl.BlockSpec((m, D_MODEL), lambda j: (0, 0)),
            pl.BlockSpec((D_MODEL, tn), lambda j: (0, j)),
            pl.BlockSpec((1, tn), lambda j: (0, j)),
        ],
        out_specs=pl.BlockSpec((m, tn), lambda j: (0, j)),
        out_shape=jax.ShapeDtypeStruct((m, n), F32),
        compiler_params=_cparams(("arbitrary",)),
        name="adaln",
    )(c, w, b)


def _merged_mixer(gtm_ref, oa_ref, or_ref, gr_ref, ga_ref, gb_ref, wm_ref):
    gr = gr_ref[...].astype(F32)
    o_r = or_ref[...].astype(F32) * (gr * jax.nn.sigmoid(gr))
    merged = (jax.nn.sigmoid(ga_ref[...].astype(F32)) * oa_ref[...].astype(F32)
              + jax.nn.sigmoid(gb_ref[...].astype(F32)) * o_r)
    return gtm_ref[...] * jnp.dot(merged.astype(BF16), wm_ref[...], preferred_element_type=F32)


def _ffn_kernel(*refs, n_ff, final_norm, merge):
    if merge:
        (x_ref, gtm_ref, oa_ref, or_ref, gr_ref, ga_ref, gb_ref, wm_ref), refs = refs[:8], refs[8:]
        sh_ref, sc_ref, gt_ref, g_ref, wa_ref, wb_ref, wo_ref, nf_ref, o_ref, h_ref, acc_ref, res_ref = refs
    else:
        x_ref, sh_ref, sc_ref, gt_ref, g_ref, wa_ref, wb_ref, wo_ref, nf_ref, o_ref, h_ref, acc_ref = refs
        res_ref = x_ref
    j = pl.program_id(1)

    @pl.when(j == 0)
    def _():
        x = x_ref[...]
        if merge:
            x = x + _merged_mixer(gtm_ref, oa_ref, or_ref, gr_ref, ga_ref, gb_ref, wm_ref)
            res_ref[...] = x
        h = _rms(x, g_ref[...], NORM_EPS) * (1.0 + sc_ref[...]) + sh_ref[...]
        h_ref[...] = h.astype(BF16)

    h = h_ref[...]
    a = jnp.dot(h, wa_ref[...], preferred_element_type=F32)
    b = jnp.dot(h, wb_ref[...], preferred_element_type=F32)
    act = (a * jax.nn.sigmoid(a) * b).astype(BF16)
    part = jnp.dot(act, wo_ref[...], preferred_element_type=F32)

    @pl.when(j == 0)
    def _():
        acc_ref[...] = part

    @pl.when(j > 0)
    def _():
        acc_ref[...] += part

    @pl.when(j == n_ff - 1)
    def _():
        out = res_ref[...] + HALF_STEP * gt_ref[...] * acc_ref[...]
        if final_norm:
            out = _rms(out, nf_ref[...], NORM_EPS)
        o_ref[...] = out


def _ffn(x, mod, mod_kind, rows_per_mod, pieces, norm_g, w_in, w_out, norm_final, final_norm, mixer=None):
    rows = x.shape[0]
    tm = min(FFN_TM, rows)
    if mixer is not None and mixer[1].dtype == F32:
        tm = tm // 2
    tf = FFN_TF
    n_ff = D_FF // tf
    row = lambda i, j: (i, 0)
    const = lambda i, j: (0, 0)
    rowspec = pl.BlockSpec((tm, D_MODEL), row)
    in_specs = [rowspec]
    args = [x]
    scratch = [pltpu.VMEM((tm, D_MODEL), BF16), pltpu.VMEM((tm, D_MODEL), F32)]
    if mixer is not None:
        in_specs += [_mod_spec(mod_kind, tm, rows_per_mod, mixer[0])] + [rowspec] * 5 + [
            pl.BlockSpec((D_MODEL, D_MODEL), const, pipeline_mode=pl.Buffered(1))]
        args += [mod, *mixer[1:]]
        scratch.append(pltpu.VMEM((tm, D_MODEL), F32))
    in_specs += [
        _mod_spec(mod_kind, tm, rows_per_mod, pieces[0]),
        _mod_spec(mod_kind, tm, rows_per_mod, pieces[1]),
        _mod_spec(mod_kind, tm, rows_per_mod, pieces[2]),
        pl.BlockSpec((1, D_MODEL), const),
        pl.BlockSpec((D_MODEL, tf), lambda i, j: (0, j)),
        pl.BlockSpec((D_MODEL, tf), lambda i, j: (0, n_ff + j)),
        pl.BlockSpec((tf, D_MODEL), lambda i, j: (j, 0)),
        pl.BlockSpec((1, D_MODEL), const),
    ]
    args += [mod, mod, mod, norm_g, w_in, w_in, w_out, norm_final]
    return pl.pallas_call(
        functools.partial(_ffn_kernel, n_ff=n_ff, final_norm=final_norm, merge=mixer is not None),
        grid=(rows // tm, n_ff),
        in_specs=in_specs,
        out_specs=rowspec,
        out_shape=jax.ShapeDtypeStruct((rows, D_MODEL), F32),
        scratch_shapes=scratch,
        compiler_params=_cparams(("parallel", "arbitrary")),
        name="ffn",
    )(*args)


def _table_kernel(inv_r_ref, sgn_r_ref, inv_t_ref, sgn_t_ref, cr_ref, sr_ref, ct_ref, st_ref,
                  lcr_ref, lsr_ref, lct_ref, lst_ref, *, offset):
    rows = cr_ref.shape[0]
    i = pl.program_id(0)

    @pl.when(i == 0)
    def _():
        local = lax.broadcasted_iota(jnp.int32, (rows, 1), 0).astype(F32)
        ang = local * inv_r_ref[...]
        lcr_ref[...] = jnp.cos(ang)
        lsr_ref[...] = jnp.sin(ang)
        ang = local * inv_t_ref[...]
        lct_ref[...] = jnp.cos(ang)
        lst_ref[...] = jnp.sin(ang)

    base = jnp.full((1, 1), i * rows + offset, jnp.int32).astype(F32)
    for inv_ref, sgn_ref, lc_ref, ls_ref, c_ref, s_ref in (
            (inv_r_ref, sgn_r_ref, lcr_ref, lsr_ref, cr_ref, sr_ref),
            (inv_t_ref, sgn_t_ref, lct_ref, lst_ref, ct_ref, st_ref)):
        ang = base * inv_ref[...]
        cb, sb = jnp.cos(ang), jnp.sin(ang)
        lc, ls = lc_ref[...], ls_ref[...]
        c_ref[...] = lc * cb - ls * sb
        s_ref[...] = (ls * cb + lc * sb) * sgn_ref[...]


def _rotation_tables(n_pos, offset):
    lane = jnp.arange(LANES)
    half = DA_DK // 2
    inv_r = (ROPE_THETA ** (-((lane % DA_DK) % half).astype(F32) / half))[None, :]
    sgn_r = jnp.where((lane % DA_DK) < half, -1.0, 1.0).astype(F32)[None, :]
    lane_t = jnp.arange(RET_DK)
    angle = 1.0 / (RET_THETA ** jnp.linspace(0.0, 1.0, RET_DK // 2, dtype=F32))
    inv_t = angle[lane_t // 2][None, :]
    sgn_t = jnp.where(lane_t % 2 == 0, -1.0, 1.0).astype(F32)[None, :]
    rows = min(n_pos, 512)
    vec = lambda w: pl.BlockSpec((1, w), lambda i: (0, 0))
    tab = lambda w: pl.BlockSpec((rows, w), lambda i: (i, 0))
    return pl.pallas_call(
        functools.partial(_table_kernel, offset=offset),
        grid=(n_pos // rows,),
        in_specs=[vec(LANES), vec(LANES), vec(RET_DK), vec(RET_DK)],
        out_specs=[tab(LANES), tab(LANES), tab(RET_DK), tab(RET_DK)],
        out_shape=[jax.ShapeDtypeStruct((n_pos, LANES), F32), jax.ShapeDtypeStruct((n_pos, LANES), F32),
                   jax.ShapeDtypeStruct((n_pos, RET_DK), F32), jax.ShapeDtypeStruct((n_pos, RET_DK), F32)],
        scratch_shapes=[pltpu.VMEM((rows, LANES), F32), pltpu.VMEM((rows, LANES), F32),
                        pltpu.VMEM((rows, RET_DK), F32), pltpu.VMEM((rows, RET_DK), F32)],
        compiler_params=_cparams(("arbitrary",)),
        name="rotation_tables",
    )(inv_r, sgn_r, inv_t, sgn_t)


def _pair_rotate(x, cos, sin_signed, shift):
    lane = lax.broadcasted_iota(jnp.int32, x.shape, 1)
    partner = jnp.where((lane % (2 * shift)) < shift, pltpu.roll(x, LANES - shift, 1), pltpu.roll(x, shift, 1))
    return x * cos + partner * sin_signed


def _proj_kernel(x_ref, sh_ref, sc_ref, g_ref, w_ref, cr_ref, sr_ref, ct_ref, st_ref, *out_refs, transposed):
    h = (_rms(x_ref[...], g_ref[...], NORM_EPS) * (1.0 + sc_ref[...]) + sh_ref[...]).astype(BF16)
    cr, sr = cr_ref[...], sr_ref[...]
    n_chunks = D_MODEL // LANES
    q_scale = (DA_DK ** -0.5) * LOG2E
    k_scale = RET_DK ** -0.5

    def group(gi):
        return jnp.dot(h, w_ref[:, gi * D_MODEL:(gi + 1) * D_MODEL], preferred_element_type=F32)

    def chunk(p, c):
        return p[:, c * LANES:(c + 1) * LANES]

    def ret_tab(t_ref, c):
        half = (c % (RET_DK // LANES)) * LANES
        return t_ref[:, half:half + LANES]

    if transposed:
        (q_ref, ktf_ref, ktb_ref, vf_ref, vb_ref, qr_ref, krt_ref, vr_ref, gr_ref, ga_ref, gb_ref) = out_refs
    else:
        (q_ref, k_ref, v_ref, qr_ref, kr_ref, vr_ref, gr_ref, ga_ref, gb_ref) = out_refs

    p = group(0)
    for c in range(n_chunks):
        q_ref[:, c * LANES:(c + 1) * LANES] = (_pair_rotate(chunk(p, c), cr, sr, DA_DK // 2) * q_scale).astype(q_ref.dtype)
    p = group(1)
    for c in range(n_chunks):
        kc = _pair_rotate(chunk(p, c), cr, sr, DA_DK // 2)
        if transposed:
            kt = kc.T
            ktf_ref[c * LANES:(c + 1) * LANES, :] = kt
            ktb_ref[c * LANES:(c + 1) * LANES, :] = kt.astype(BF16)
        else:
            k_ref[:, c * LANES:(c + 1) * LANES] = kc
    p = group(2)
    if transposed:
        vf_ref[...] = p
        ones = jnp.ones((p.shape[0], DA_DV), BF16)
        for hd in range(DA_HEADS):
            vb_ref[:, 2 * hd * DA_DV:(2 * hd + 1) * DA_DV] = chunk(p, hd).astype(BF16)
            vb_ref[:, (2 * hd + 1) * DA_DV:(2 * hd + 2) * DA_DV] = ones
    else:
        v_ref[...] = p
    p = group(3)
    for c in range(n_chunks):
        qc = _pair_rotate(chunk(p, c), ret_tab(ct_ref, c), ret_tab(st_ref, c), 1)
        qr_ref[:, c * LANES:(c + 1) * LANES] = qc.astype(qr_ref.dtype)
    p = group(4)
    for c in range(n_chunks):
        kc = _pair_rotate(chunk(p, c), ret_tab(ct_ref, c), ret_tab(st_ref, c), 1) * k_scale
        if transposed:
            krt_ref[c * LANES:(c + 1) * LANES, :] = kc.T.astype(BF16)
        else:
            kr_ref[:, c * LANES:(c + 1) * LANES] = kc
    for gi, ref in ((5, vr_ref), (6, gr_ref), (7, ga_ref), (8, gb_ref)):
        ref[...] = group(gi).astype(ref.dtype)


def _proj(x, mod, mod_kind, rows_per_mod, norm_g, w_in, tables, n_tab_tiles, transposed, batch, seq):
    rows = x.shape[0]
    tm = min(PROJ_TM, rows)
    row = lambda i: (i, 0)
    const = lambda i: (0, 0)
    tab = lambda w: pl.BlockSpec((tm, w), lambda i: (i % n_tab_tiles, 0))
    rowspec = pl.BlockSpec((tm, D_MODEL), row)
    if transposed:
        tiles = seq // tm
        tspec = pl.BlockSpec((None, D_MODEL, tm), lambda i: (i // tiles, 0, i % tiles))
        t_shape = lambda dt: jax.ShapeDtypeStruct((batch, D_MODEL, seq), dt)
        r_shape = lambda dt: jax.ShapeDtypeStruct((rows, D_MODEL), dt)
        vspec = pl.BlockSpec((tm, ATTN_V_WIDTH), row)
        v_shape = jax.ShapeDtypeStruct((rows, ATTN_V_WIDTH), BF16)
        out_specs = [rowspec, tspec, tspec, rowspec, vspec, rowspec, tspec, rowspec, rowspec, rowspec, rowspec]
        out_shape = [r_shape(BF16), t_shape(F32), t_shape(BF16), r_shape(F32), v_shape, r_shape(BF16),
                     t_shape(BF16), r_shape(BF16), r_shape(BF16), r_shape(BF16), r_shape(BF16)]
    else:
        out_specs = [rowspec] * N_PROJ
        out_shape = [jax.ShapeDtypeStruct((rows, D_MODEL), F32)] * N_PROJ
    return pl.pallas_call(
        functools.partial(_proj_kernel, transposed=transposed),
        grid=(rows // tm,),
        in_specs=[
            rowspec,
            _mod_spec(mod_kind, tm, rows_per_mod, 3),
            _mod_spec(mod_kind, tm, rows_per_mod, 4),
            pl.BlockSpec((1, D_MODEL), const),
            pl.BlockSpec((D_MODEL, N_PROJ * D_MODEL), const, pipeline_mode=pl.Buffered(1)),
            tab(LANES), tab(LANES), tab(RET_DK), tab(RET_DK),
        ],
        out_specs=out_specs,
        out_shape=out_shape,
        compiler_params=_cparams(("parallel",)),
        name="mixer_proj",
    )(x, mod, mod, norm_g, w_in, *tables)


def _lambda(lq1, lk1, lq2, lk2, lam_init):
    return (jnp.exp(jnp.sum(lq1 * lk1, axis=-1, keepdims=True))
            - jnp.exp(jnp.sum(lq2 * lk2, axis=-1, keepdims=True)) + lam_init)


def _attn_kernel(qi_ref, kj_ref, q_ref, kt_ref, v_ref, lq1_ref, lk1_ref, lq2_ref, lk2_ref, sg_ref, o_ref,
                 qz_ref, m_ref, acc_ref, *, tq, lam_init):
    step = pl.program_id(1)
    qi = qi_ref[step]
    kj = kj_ref[step]
    lane = lax.broadcasted_iota(jnp.int32, (tq, DA_DV), 1)
    n_rep = tq // LANES

    @pl.when(kj == 0)
    def _():
        for h in range(DA_HEADS):
            qh = q_ref[:, h * DA_DV:(h + 1) * DA_DV]
            qz_ref[h, :tq, :] = jnp.where(lane < DA_DK, qh, jnp.zeros_like(qh))
            qz_ref[h, tq:, :] = jnp.where(lane >= DA_DK, qh, jnp.zeros_like(qh))
        m_ref[...] = jnp.full(m_ref.shape, NEG_INF, F32)
        acc_ref[...] = jnp.zeros(acc_ref.shape, F32)

    def block(masked):
        if masked:
            r = lax.broadcasted_iota(jnp.int32, (2 * tq, tq), 0) % tq
            c = lax.broadcasted_iota(jnp.int32, (2 * tq, tq), 1)
            keep = c <= r
        for h in range(DA_HEADS):
            s = jnp.dot(qz_ref[h], kt_ref[h * DA_DV:(h + 1) * DA_DV, :], preferred_element_type=F32)
            if masked:
                s = jnp.where(keep, s, NEG_INF)
            m_old = m_ref[h]
            m_new = jnp.maximum(m_old, jnp.max(s, axis=-1, keepdims=True))
            alpha = jnp.exp2(m_old - m_new)
            p = jnp.exp2(s - jnp.concatenate([m_new] * n_rep, axis=1))
            pv = jnp.dot(p.astype(BF16), v_ref[:, h * 2 * DA_DV:(h + 1) * 2 * DA_DV], preferred_element_type=F32)
            acc_ref[h] = jnp.concatenate([alpha, alpha], axis=1) * acc_ref[h] + pv
            m_ref[h] = m_new

    @pl.when(kj < qi)
    def _():
        block(False)

    @pl.when(kj == qi)
    def _():
        block(True)
        lam = _lambda(lq1_ref[...], lk1_ref[...], lq2_ref[...], lk2_ref[...], lam_init)
        for h in range(DA_HEADS):
            o = acc_ref[h, :, :DA_DV] / acc_ref[h, :, DA_DV:]
            d = o[:tq] - lam * o[tq:]
            o_ref[:, h * DA_DV:(h + 1) * DA_DV] = (_rms(d, sg_ref[...], SUBLN_EPS) * (1.0 - lam_init)).astype(o_ref.dtype)


def _attn_prompt(q, kt, v, lam_vecs, subln_g, batch, seq, lam_init):
    tq = min(ATTN_TQ, seq)
    nq = seq // tq
    pairs = [(i, j) for i in range(nq) for j in range(i + 1)]
    qi = jnp.asarray([p[0] for p in pairs], jnp.int32)
    kj = jnp.asarray([p[1] for p in pairs], jnp.int32)
    vec = lambda w: pl.BlockSpec((1, w), lambda b, s, qi, kj: (0, 0))
    grid_spec = pltpu.PrefetchScalarGridSpec(
        num_scalar_prefetch=2,
        grid=(batch, len(pairs)),
        in_specs=[
            pl.BlockSpec((tq, D_MODEL), lambda b, s, qi, kj: (b * nq + qi[s], 0)),
            pl.BlockSpec((None, D_MODEL, tq), lambda b, s, qi, kj: (b, 0, kj[s])),
            pl.BlockSpec((tq, ATTN_V_WIDTH), lambda b, s, qi, kj: (b * nq + kj[s], 0)),
            vec(DA_DK), vec(DA_DK), vec(DA_DK), vec(DA_DK), vec(DA_DV),
        ],
        out_specs=pl.BlockSpec((tq, D_MODEL), lambda b, s, qi, kj: (b * nq + qi[s], 0)),
        scratch_shapes=[
            pltpu.VMEM((DA_HEADS, 2 * tq, DA_DV), BF16),
            pltpu.VMEM((DA_HEADS, 2 * tq, LANES), F32),
            pltpu.VMEM((DA_HEADS, 2 * tq, 2 * DA_DV), F32),
        ],
    )
    return pl.pallas_call(
        functools.partial(_attn_kernel, tq=tq, lam_init=lam_init),
        grid_spec=grid_spec,
        out_shape=jax.ShapeDtypeStruct((batch * seq, D_MODEL), BF16),
        compiler_params=_cparams(("parallel", "arbitrary")),
        name="diff_attn_prompt",
    )(qi, kj, q, kt, v, *lam_vecs, subln_g)


def _attn_sample_kernel(pt_ref, *refs, n_pages, page, dec_seq, lam_init):
    k_refs = refs[:n_pages]
    v_refs = refs[n_pages:2 * n_pages]
    (q_ref, kn_ref, vn_ref, lq1_ref, lk1_ref, lq2_ref, lk2_ref, sg_ref, o_ref, s_ref) = refs[2 * n_pages:]
    del pt_ref
    rows = 2 * dec_seq
    q = q_ref[...].astype(BF16)
    for i in range(n_pages):
        s_ref[:, :, i * page:(i + 1) * page] = jnp.einsum(
            "hrk,hkt->hrt", q, k_refs[i][...].astype(BF16), preferred_element_type=F32)
    qf = q.astype(F32)
    kn = kn_ref[...].astype(BF16).astype(F32)
    vn = vn_ref[...].astype(BF16).astype(F32)
    r = lax.broadcasted_iota(jnp.int32, (DA_HEADS, rows, 1), 1) % dec_seq
    s_new = [jnp.where(r >= t, jnp.sum(qf * kn[:, t:t + 1, :], axis=-1, keepdims=True), NEG_INF)
             for t in range(dec_seq)]
    s_old = s_ref[...]
    m = jnp.max(s_old, axis=-1, keepdims=True)
    for s_t in s_new:
        m = jnp.maximum(m, s_t)
    p_old = jnp.exp2(s_old - m)
    denom = jnp.sum(p_old, axis=-1, keepdims=True)
    acc_new = jnp.zeros((DA_HEADS, rows, DA_DV), F32)
    for t, s_t in enumerate(s_new):
        p_t = jnp.exp2(s_t - m)
        denom = denom + p_t
        acc_new = acc_new + p_t.astype(BF16).astype(F32) * vn[:, t:t + 1, :]
    s_ref[...] = p_old
    lam = _lambda(lq1_ref[...], lk1_ref[...], lq2_ref[...], lk2_ref[...], lam_init)
    for h in range(DA_HEADS):
        acc = acc_new[h]
        for i in range(n_pages):
            acc = acc + jnp.dot(s_ref[h, :, i * page:(i + 1) * page].astype(BF16),
                                v_refs[i][pl.ds(h, page, stride=DA_HEADS), :].astype(BF16),
                                preferred_element_type=F32)
        o = acc / denom[h]
        d = o[:dec_seq] - lam * o[dec_seq:]
        o_ref[:, h * DA_DV:(h + 1) * DA_DV] = _rms(d, sg_ref[...], SUBLN_EPS) * (1.0 - lam_init)


def _attn_sample(page_table, cache_kt, cache_v, q_bd, k_new, v_new, lam_vecs, subln_g, lam_init):
    dec_batch, n_pages = page_table.shape
    page = cache_v.shape[1]
    dec_seq = k_new.shape[2]
    kspec = lambda i: pl.BlockSpec((None, DA_HEADS, DA_DV, page), lambda b, pt: (pt[b, i], 0, 0, 0))
    cache_v = cache_v.reshape(cache_v.shape[0], page * DA_HEADS, DA_DV)
    vspec = lambda i: pl.BlockSpec((None, page * DA_HEADS, DA_DV), lambda b, pt: (pt[b, i], 0, 0))
    per_b = lambda r: pl.BlockSpec((None, DA_HEADS, r, DA_DV), lambda b, pt: (b, 0, 0, 0))
    vec = lambda w: pl.BlockSpec((1, w), lambda b, pt: (0, 0))
    grid_spec = pltpu.PrefetchScalarGridSpec(
        num_scalar_prefetch=1,
        grid=(dec_batch,),
        in_specs=[kspec(i) for i in range(n_pages)] + [vspec(i) for i in range(n_pages)] + [
            per_b(2 * dec_seq), per_b(dec_seq), per_b(dec_seq),
            vec(DA_DK), vec(DA_DK), vec(DA_DK), vec(DA_DK), vec(DA_DV)],
        out_specs=pl.BlockSpec((None, dec_seq, D_MODEL), lambda b, pt: (b, 0, 0)),
        scratch_shapes=[pltpu.VMEM((DA_HEADS, 2 * dec_seq, n_pages * page), F32)],
    )
    return pl.pallas_call(
        functools.partial(_attn_sample_kernel, n_pages=n_pages, page=page, dec_seq=dec_seq, lam_init=lam_init),
        grid_spec=grid_spec,
        out_shape=jax.ShapeDtypeStruct((dec_batch, dec_seq, D_MODEL), F32),
        compiler_params=_cparams(("parallel",)),
        name="diff_attn_sample",
    )(page_table, *([cache_kt] * n_pages), *([cache_v] * n_pages), q_bd, k_new, v_new, *lam_vecs, subln_g)


def _group_norm(o, g):
    mu = jnp.mean(o, axis=-1, keepdims=True)
    d = o - mu
    return d * lax.rsqrt(jnp.mean(d * d, axis=-1, keepdims=True) + GN_EPS) * g


def _ret_prompt_kernel(lg_ref, q_ref, kt_ref, v_ref, g_ref, o_ref, st_ref, decay_ref, *, chunk):
    c = pl.program_id(0)
    batch = q_ref.shape[0]

    @pl.when(c == 0)
    def _():
        st_ref[...] = jnp.zeros(st_ref.shape, F32)
        ri = lax.broadcasted_iota(jnp.int32, (chunk, chunk), 0)
        ci = lax.broadcasted_iota(jnp.int32, (chunk, chunk), 1)
        dist = (ri - ci).astype(F32)
        for h in range(RET_HEADS):
            decay_ref[h] = jnp.where(dist >= 0, jnp.exp(jnp.maximum(dist, 0.0) * lg_ref[h]), 0.0)

    row = lax.broadcasted_iota(jnp.int32, (chunk, 1), 0)
    col = lax.broadcasted_iota(jnp.int32, (1, chunk), 1)
    for h in range(RET_HEADS):
        lg = lg_ref[h]
        sl = slice(h * RET_DK, (h + 1) * RET_DK)
        q_decay = jnp.exp((row + 1).astype(F32) * lg)
        k_decay = jnp.exp((chunk - 1 - col).astype(F32) * lg)
        carry = jnp.exp(jnp.full((1, 1), chunk, F32) * lg)
        for b in range(batch):
            q = q_ref[b, :, sl]
            kt = kt_ref[b, sl, :]
            v = v_ref[b, :, sl]
            scores = jnp.dot(q, kt, preferred_element_type=F32) * decay_ref[h]
            inner = jnp.dot(scores.astype(BF16), v, preferred_element_type=F32)
            state = st_ref[b, h]
            cross = jnp.dot(q, state.astype(BF16), preferred_element_type=F32) * q_decay
            o_ref[b, :, sl] = _group_norm(inner + cross, g_ref[:, sl]).astype(o_ref.dtype)
            ktd = (kt.astype(F32) * k_decay).astype(BF16)
            st_ref[b, h] = carry * state + jnp.dot(ktd, v, preferred_element_type=F32)


def _ret_prompt(log_g, q, kt, v, gn_g, batch, seq):
    chunk = min(RET_L, seq)
    tok = pl.BlockSpec((batch, chunk, D_MODEL), lambda c: (0, c, 0))
    o_r, state = pl.pallas_call(
        functools.partial(_ret_prompt_kernel, chunk=chunk),
        grid=(seq // chunk,),
        in_specs=[
            pl.BlockSpec(memory_space=pltpu.SMEM),
            tok,
            pl.BlockSpec((batch, D_MODEL, chunk), lambda c: (0, 0, c)),
            tok,
            pl.BlockSpec((1, D_MODEL), lambda c: (0, 0)),
        ],
        out_specs=[tok, pl.BlockSpec((batch, RET_HEADS, RET_DK, RET_DV), lambda c: (0, 0, 0, 0))],
        out_shape=[jax.ShapeDtypeStruct((batch, seq, D_MODEL), BF16),
                   jax.ShapeDtypeStruct((batch, RET_HEADS, RET_DK, RET_DV), F32)],
        scratch_shapes=[pltpu.VMEM((RET_HEADS, chunk, chunk), F32)],
        compiler_params=_cparams(("arbitrary",)),
        name="retention_prompt",
    )(log_g, q.reshape(batch, seq, D_MODEL), kt, v.reshape(batch, seq, D_MODEL), gn_g)
    return o_r.reshape(batch * seq, D_MODEL), state


def _ret_sample_kernel(lg_ref, q_ref, k_ref, v_ref, s_ref, g_ref, o_ref, st_ref, *, dec_seq):
    n_b, rows = q_ref.shape[:2]
    ri = lax.broadcasted_iota(jnp.int32, (rows, 1), 0)
    zpad = jnp.zeros((LANES - rows, RET_DK), F32)
    for h in range(RET_HEADS):
        lg = lg_ref[h]
        sl = slice(h * RET_DK, (h + 1) * RET_DK)
        decay = []
        for j in range(dec_seq):
            dist = (ri - j).astype(F32)
            decay.append(jnp.where(dist >= 0, jnp.exp(jnp.maximum(dist, 0.0) * lg), 0.0))
        q_decay = jnp.exp((ri + 1).astype(F32) * lg)
        k_decay = jnp.where(ri < dec_seq, jnp.exp((dec_seq - 1 - ri).astype(F32) * lg), 0.0)
        carry = jnp.exp(jnp.full((1, 1), dec_seq, F32) * lg)
        for b in range(n_b):
            q = q_ref[b, :, sl]
            k = k_ref[b, :, sl]
            v = v_ref[b, :, sl]
            qb = q.astype(BF16).astype(F32)
            kb = k.astype(BF16).astype(F32)
            vb = v.astype(BF16).astype(F32)
            state = s_ref[b, h]
            inner = jnp.zeros((rows, RET_DV), F32)
            for j in range(dec_seq):
                score = jnp.sum(qb * kb[j:j + 1, :], axis=-1, keepdims=True) * decay[j]
                inner = inner + score.astype(BF16).astype(F32) * vb[j:j + 1, :]
            cross = jnp.dot(q.astype(BF16), state.astype(BF16), preferred_element_type=F32) * q_decay
            o_ref[b, :, sl] = _group_norm(inner + cross, g_ref[:, sl])
            kd_t = jnp.concatenate([k * k_decay, zpad], axis=0).T.astype(BF16)
            v_pad = jnp.concatenate([v, zpad], axis=0).astype(BF16)
            st_ref[b, h] = carry * state + jnp.dot(kd_t, v_pad, preferred_element_type=F32)


def _ret_sample(log_g, q, k, v, state, gn_g, dec_seq):
    dec_batch, rows = q.shape[:2]
    n_b = math.gcd(dec_batch, RET_SAMPLE_NB)
    tok = pl.BlockSpec((n_b, rows, D_MODEL), lambda b: (b, 0, 0))
    st = pl.BlockSpec((n_b, RET_HEADS, RET_DK, RET_DV), lambda b: (b, 0, 0, 0))
    return pl.pallas_call(
        functools.partial(_ret_sample_kernel, dec_seq=dec_seq),
        grid=(dec_batch // n_b,),
        in_specs=[pl.BlockSpec(memory_space=pltpu.SMEM), tok, tok, tok, st,
                  pl.BlockSpec((1, D_MODEL), lambda b: (0, 0))],
        out_specs=[tok, st],
        out_shape=[jax.ShapeDtypeStruct((dec_batch, rows, D_MODEL), F32),
                   jax.ShapeDtypeStruct(state.shape, F32)],
        compiler_params=_cparams(("parallel",)),
        name="retention_sample",
    )(log_g, q, k, v, state, gn_g)


def kernel(x_prompt, x_sample, cache_k, cache_v, state_ret, page_table, c_prompt, c_sample, ada_w, ada_b, norm_ffn1, norm_mix, norm_ffn2, ffn1_w_in, ffn1_w_out, ffn2_w_in, ffn2_w_out, w_in, w_out, lam_q1, lam_k1, lam_q2, lam_k2, subln_g, ret_norm_g, norm_final):
    batch, seq, _ = x_prompt.shape
    dec_batch, dec_seq, _ = x_sample.shape
    depth = ada_w.shape[0]
    page = cache_k.shape[2]
    past_len = page_table.shape[1] * page
    log_g = jnp.log1p(-jnp.exp2(-5.0 - jnp.arange(RET_HEADS, dtype=F32)))
    nf = norm_final[None, :]

    tab_p = _rotation_tables(seq, 0)
    sub = 8
    tab_s = [jnp.tile(t[:dec_seq], (dec_batch, 1)) for t in _rotation_tables(sub, past_len)]

    rows_s = dec_batch * dec_seq
    n_c = rows_s + batch
    c_all = jnp.concatenate([jnp.repeat(c_sample, dec_seq, axis=0), c_prompt,
                             jnp.zeros((-n_c % sub, D_MODEL), F32)], axis=0)

    yp = x_prompt.reshape(batch * seq, D_MODEL)
    ys = x_sample.reshape(dec_batch * dec_seq, D_MODEL)
    kp_l, vp_l, sp_l, ks_l, vs_l, ss_l = [], [], [], [], [], []
    for l in range(depth):
        lam_init = 0.8 - 0.6 * math.exp(-0.3 * l)
        mod_s = _adaln(c_all, ada_w[l], ada_b[l][None, :])
        mod_p = mod_s[rows_s:n_c].reshape(batch, 1, N_MOD * D_MODEL)
        w1i, w1o = ffn1_w_in[l].astype(BF16), ffn1_w_out[l].astype(BF16)
        w2i, w2o = ffn2_w_in[l].astype(BF16), ffn2_w_out[l].astype(BF16)
        wi, wo = w_in[l].astype(BF16), w_out[l].astype(BF16)
        g1, gm, g2 = norm_ffn1[l][None, :], norm_mix[l][None, :], norm_ffn2[l][None, :]
        lam_vecs = (lam_q1[l][None, :], lam_k1[l][None, :], lam_q2[l][None, :], lam_k2[l][None, :])
        sg = subln_g[l][None, :]
        gn = ret_norm_g[l][None, :]
        last = l == depth - 1

        x1 = _ffn(yp, mod_p, "batch", seq, (0, 1, 2), g1, w1i, w1o, nf, False)
        (q, ktf, ktb, vf, vb, qr, krt, vr, gr, ga, gb) = _proj(
            x1, mod_p, "batch", seq, gm, wi, tab_p, seq // min(PROJ_TM, seq), True, batch, seq)
        o_a = _attn_prompt(q, ktb, vb, lam_vecs, sg, batch, seq, lam_init)
        o_r, sp = _ret_prompt(log_g, qr, krt, vr, gn, batch, seq)
        yp = _ffn(x1, mod_p, "batch", seq, (6, 7, 8), g2, w2i, w2o, nf, last, mixer=(5, o_a, o_r, gr, ga, gb, wo))
        kp_l.append(jnp.transpose(ktf.reshape(batch, 2 * DA_HEADS, DA_DK, seq), (0, 3, 1, 2)))
        vp_l.append(vf.reshape(batch, seq, DA_HEADS, DA_DV))
        sp_l.append(sp)

        x1 = _ffn(ys, mod_s, "token", 1, (0, 1, 2), g1, w1i, w1o, nf, False)
        (q, k, v, qr, kr, vr, gr, ga, gb) = _proj(
            x1, mod_s, "token", 1, gm, wi, tab_s, rows_s // min(PROJ_TM, rows_s), False, dec_batch, dec_seq)
        qh = q.reshape(dec_batch, dec_seq, DA_HEADS, 2, DA_DK).transpose(0, 2, 3, 1, 4)
        eye = jnp.eye(2, dtype=F32)
        q_bd = (qh[:, :, :, :, None, :] * eye[None, None, :, None, :, None]).reshape(
            dec_batch, DA_HEADS, 2 * dec_seq, DA_DV)
        to_heads = lambda a: a.reshape(dec_batch, dec_seq, DA_HEADS, DA_DV).transpose(0, 2, 1, 3)
        cache_kt = jnp.transpose(cache_k[l], (0, 2, 3, 1)).reshape(-1, DA_HEADS, DA_DV, page)
        o_a = _attn_sample(page_table, cache_kt, cache_v[l], q_bd, to_heads(k), to_heads(v), lam_vecs, sg, lam_init)
        pad = lambda a: jnp.pad(a.reshape(dec_batch, dec_seq, D_MODEL), ((0, 0), (0, sub - dec_seq), (0, 0)))
        o_r, ssm = _ret_sample(log_g, pad(qr), pad(kr), pad(vr), state_ret[l], gn, dec_seq)
        o_r = o_r[:, :dec_seq].reshape(rows_s, D_MODEL)
        ys = _ffn(x1, mod_s, "token", 1, (6, 7, 8), g2, w2i, w2o, nf, last,
                  mixer=(5, o_a.reshape(rows_s, D_MODEL), o_r, gr, ga, gb, wo))
        ks_l.append(k.reshape(dec_batch, dec_seq, 2 * DA_HEADS, DA_DK))
        vs_l.append(v.reshape(dec_batch, dec_seq, DA_HEADS, DA_DV))
        ss_l.append(ssm)

    return (yp.reshape(batch, seq, D_MODEL), ys.reshape(dec_batch, dec_seq, D_MODEL),
            jnp.stack(kp_l), jnp.stack(vp_l), jnp.stack(sp_l), jnp.stack(ks_l), jnp.stack(vs_l), jnp.stack(ss_l))
```

```python
import functools
import math

import jax
import jax.numpy as jnp
from jax import lax
from jax.experimental import pallas as pl
from jax.experimental.pallas import tpu as pltpu

F32 = jnp.float32
BF16 = jnp.bfloat16

D_MODEL = 1024
DA_DK = 64
DA_DV = 128
DA_HEADS = 8
RET_HEADS = 4
RET_DK = 256
RET_DV = 256
D_FF = 2816
N_MOD = 9
N_PROJ = 9
HALF_STEP = 0.5
ROPE_THETA = 10000.0
RET_THETA = 10000.0
NORM_EPS = 1e-6
SUBLN_EPS = 1e-5
GN_EPS = 1e-5
NEG_INF = -1e30
LOG2E = 1.4426950408889634
LANES = 128
VMEM_LIMIT = 56 * 1024 * 1024

FFN_TM = 512
FFN_TF = 1408
PROJ_TM = 256
ATTN_TQ = 512
ATTN_V_WIDTH = 2 * DA_HEADS * DA_DV
RET_L = 512
RET_SAMPLE_NB = 4


def _cparams(sem):
    return pltpu.CompilerParams(dimension_semantics=sem, vmem_limit_bytes=VMEM_LIMIT)


def _rms(x, g, eps):
    return x * lax.rsqrt(jnp.mean(x * x, axis=-1, keepdims=True) + eps) * g


def _mod_spec(kind, tm, rows_per_mod, piece):
    if kind == "batch":
        return pl.BlockSpec((None, 1, D_MODEL), lambda i, *_: ((i * tm) // rows_per_mod, 0, piece))
    return pl.BlockSpec((tm, D_MODEL), lambda i, *_: (i, piece))


def _adaln_kernel(c_ref, w_ref, b_ref, o_ref):
    c = c_ref[...]
    a = (c * jax.nn.sigmoid(c)).astype(BF16)
    o_ref[...] = jnp.dot(a, w_ref[...].astype(BF16), preferred_element_type=F32) + b_ref[...]


def _adaln(c, w, b):
    m = c.shape[0]
    n = w.shape[1]
    tn = 1152
    return pl.pallas_call(
        _adaln_kernel,
        grid=(n // tn,),
        in_specs=[
            pl.BlockSpec((m, D_MODEL), lambda j: (0, 0)),
            pl.BlockSpec((D_MODEL, tn), lambda j: (0, j)),
            pl.BlockSpec((1, tn), lambda j: (0, j)),
        ],
        out_specs=pl.BlockSpec((m, tn), lambda j: (0, j)),
        out_shape=jax.ShapeDtypeStruct((m, n), F32),
        compiler_params=_cparams(("arbitrary",)),
        name="adaln",
    )(c, w, b)


def _merged_mixer(gtm_ref, oa_ref, or_ref, gr_ref, ga_ref, gb_ref, wm_ref):
    gr = gr_ref[...].astype(F32)
    o_r = or_ref[...].astype(F32) * (gr * jax.nn.sigmoid(gr))
    merged = (jax.nn.sigmoid(ga_ref[...].astype(F32)) * oa_ref[...].astype(F32)
              + jax.nn.sigmoid(gb_ref[...].astype(F32)) * o_r)
    return gtm_ref[...] * jnp.dot(merged.astype(BF16), wm_ref[...], preferred_element_type=F32)


def _ffn_kernel(*refs, final_norm, merge):
    if merge:
        (x_ref, gtm_ref, oa_ref, or_ref, gr_ref, ga_ref, gb_ref, wm_ref), refs = refs[:8], refs[8:]
    else:
        x_ref, refs = refs[0], refs[1:]
    sh_ref, sc_ref, gt_ref, g_ref, wi_ref, wo_ref, nf_ref, o_ref = refs
    x = x_ref[...]
    if merge:
        x = x + _merged_mixer(gtm_ref, oa_ref, or_ref, gr_ref, ga_ref, gb_ref, wm_ref)
    h = (_rms(x, g_ref[...], NORM_EPS) * (1.0 + sc_ref[...]) + sh_ref[...]).astype(BF16)
    acc = None
    for c in range(D_FF // FFN_TF):
        lo = c * FFN_TF
        a = jnp.dot(h, wi_ref[:, lo:lo + FFN_TF], preferred_element_type=F32)
        b = jnp.dot(h, wi_ref[:, D_FF + lo:D_FF + lo + FFN_TF], preferred_element_type=F32)
        act = (a * jax.nn.sigmoid(a) * b).astype(BF16)
        part = jnp.dot(act, wo_ref[lo:lo + FFN_TF, :], preferred_element_type=F32)
        acc = part if acc is None else acc + part
    out = x + HALF_STEP * gt_ref[...] * acc
    if final_norm:
        out = _rms(out, nf_ref[...], NORM_EPS)
    o_ref[...] = out


def _ffn(x, mod, mod_kind, rows_per_mod, pieces, norm_g, w_in, w_out, norm_final, final_norm, mixer=None):
    rows = x.shape[0]
    tm = min(FFN_TM, rows)
    if mixer is not None and mixer[1].dtype == F32:
        tm = tm // 2
    const = lambda i: (0, 0)
    rowspec = pl.BlockSpec((tm, D_MODEL), lambda i: (i, 0))
    resident = lambda shape: pl.BlockSpec(shape, const, pipeline_mode=pl.Buffered(1))
    in_specs = [rowspec]
    args = [x]
    if mixer is not None:
        in_specs += [_mod_spec(mod_kind, tm, rows_per_mod, mixer[0])] + [rowspec] * 5 + [
            resident((D_MODEL, D_MODEL))]
        args += [mod, *mixer[1:]]
    in_specs += [
        _mod_spec(mod_kind, tm, rows_per_mod, pieces[0]),
        _mod_spec(mod_kind, tm, rows_per_mod, pieces[1]),
        _mod_spec(mod_kind, tm, rows_per_mod, pieces[2]),
        pl.BlockSpec((1, D_MODEL), const),
        resident((D_MODEL, 2 * D_FF)),
        resident((D_FF, D_MODEL)),
        pl.BlockSpec((1, D_MODEL), const),
    ]
    args += [mod, mod, mod, norm_g, w_in, w_out, norm_final]
    return pl.pallas_call(
        functools.partial(_ffn_kernel, final_norm=final_norm, merge=mixer is not None),
        grid=(rows // tm,),
        in_specs=in_specs,
        out_specs=rowspec,
        out_shape=jax.ShapeDtypeStruct((rows, D_MODEL), F32),
        compiler_params=_cparams(("parallel",)),
        name="ffn",
    )(*args)


def _table_kernel(inv_r_ref, sgn_r_ref, inv_t_ref, sgn_t_ref, cr_ref, sr_ref, ct_ref, st_ref,
                  lcr_ref, lsr_ref, lct_ref, lst_ref, *, offset):
    rows = cr_ref.shape[0]
    i = pl.program_id(0)

    @pl.when(i == 0)
    def _():
        local = lax.broadcasted_iota(jnp.int32, (rows, 1), 0).astype(F32)
        ang = local * inv_r_ref[...]
        lcr_ref[...] = jnp.cos(ang)
        lsr_ref[...] = jnp.sin(ang)
        ang = local * inv_t_ref[...]
        lct_ref[...] = jnp.cos(ang)
        lst_ref[...] = jnp.sin(ang)

    base = jnp.full((1, 1), i * rows + offset, jnp.int32).astype(F32)
    for inv_ref, sgn_ref, lc_ref, ls_ref, c_ref, s_ref in (
            (inv_r_ref, sgn_r_ref, lcr_ref, lsr_ref, cr_ref, sr_ref),
            (inv_t_ref, sgn_t_ref, lct_ref, lst_ref, ct_ref, st_ref)):
        ang = base * inv_ref[...]
        cb, sb = jnp.cos(ang), jnp.sin(ang)
        lc, ls = lc_ref[...], ls_ref[...]
        c_ref[...] = lc * cb - ls * sb
        s_ref[...] = (ls * cb + lc * sb) * sgn_ref[...]


def _rotation_tables(n_pos, offset):
    lane = jnp.arange(LANES)
    half = DA_DK // 2
    inv_r = (ROPE_THETA ** (-((lane % DA_DK) % half).astype(F32) / half))[None, :]
    sgn_r = jnp.where((lane % DA_DK) < half, -1.0, 1.0).astype(F32)[None, :]
    lane_t = jnp.arange(RET_DK)
    angle = 1.0 / (RET_THETA ** jnp.linspace(0.0, 1.0, RET_DK // 2, dtype=F32))
    inv_t = angle[lane_t // 2][None, :]
    sgn_t = jnp.where(lane_t % 2 == 0, -1.0, 1.0).astype(F32)[None, :]
    rows = min(n_pos, 512)
    vec = lambda w: pl.BlockSpec((1, w), lambda i: (0, 0))
    tab = lambda w: pl.BlockSpec((rows, w), lambda i: (i, 0))
    return pl.pallas_call(
        functools.partial(_table_kernel, offset=offset),
        grid=(n_pos // rows,),
        in_specs=[vec(LANES), vec(LANES), vec(RET_DK), vec(RET_DK)],
        out_specs=[tab(LANES), tab(LANES), tab(RET_DK), tab(RET_DK)],
        out_shape=[jax.ShapeDtypeStruct((n_pos, LANES), F32), jax.ShapeDtypeStruct((n_pos, LANES), F32),
                   jax.ShapeDtypeStruct((n_pos, RET_DK), F32), jax.ShapeDtypeStruct((n_pos, RET_DK), F32)],
        scratch_shapes=[pltpu.VMEM((rows, LANES), F32), pltpu.VMEM((rows, LANES), F32),
                        pltpu.VMEM((rows, RET_DK), F32), pltpu.VMEM((rows, RET_DK), F32)],
        compiler_params=_cparams(("arbitrary",)),
        name="rotation_tables",
    )(inv_r, sgn_r, inv_t, sgn_t)


def _pair_rotate(x, cos, sin_signed, shift):
    lane = lax.broadcasted_iota(jnp.int32, x.shape, 1)
    partner = jnp.where((lane % (2 * shift)) < shift, pltpu.roll(x, LANES - shift, 1), pltpu.roll(x, shift, 1))
    return x * cos + partner * sin_signed


def _proj_kernel(x_ref, sh_ref, sc_ref, g_ref, w_ref, cr_ref, sr_ref, ct_ref, st_ref, *out_refs, transposed):
    h = (_rms(x_ref[...], g_ref[...], NORM_EPS) * (1.0 + sc_ref[...]) + sh_ref[...]).astype(BF16)
    cr, sr = cr_ref[...], sr_ref[...]
    n_chunks = D_MODEL // LANES
    q_scale = (DA_DK ** -0.5) * LOG2E
    k_scale = RET_DK ** -0.5

    def group(gi):
        return jnp.dot(h, w_ref[:, gi * D_MODEL:(gi + 1) * D_MODEL], preferred_element_type=F32)

    def chunk(p, c):
        return p[:, c * LANES:(c + 1) * LANES]

    def ret_tab(t_ref, c):
        half = (c % (RET_DK // LANES)) * LANES
        return t_ref[:, half:half + LANES]

    if transposed:
        (q_ref, ktf_ref, ktb_ref, vf_ref, vb_ref, qr_ref, krt_ref, vr_ref, gr_ref, ga_ref, gb_ref) = out_refs
    else:
        (q_ref, k_ref, v_ref, qr_ref, kr_ref, vr_ref, gr_ref, ga_ref, gb_ref) = out_refs

    p = group(0)
    for c in range(n_chunks):
        q_ref[:, c * LANES:(c + 1) * LANES] = (_pair_rotate(chunk(p, c), cr, sr, DA_DK // 2) * q_scale).astype(q_ref.dtype)
    p = group(1)
    for c in range(n_chunks):
        kc = _pair_rotate(chunk(p, c), cr, sr, DA_DK // 2)
        if transposed:
            kt = kc.T
            ktf_ref[c * LANES:(c + 1) * LANES, :] = kt
            ktb_ref[c * LANES:(c + 1) * LANES, :] = kt.astype(BF16)
        else:
            k_ref[:, c * LANES:(c + 1) * LANES] = kc
    p = group(2)
    if transposed:
        vf_ref[...] = p
        ones = jnp.ones((p.shape[0], DA_DV), BF16)
        for hd in range(DA_HEADS):
            vb_ref[:, 2 * hd * DA_DV:(2 * hd + 1) * DA_DV] = chunk(p, hd).astype(BF16)
            vb_ref[:, (2 * hd + 1) * DA_DV:(2 * hd + 2) * DA_DV] = ones
    else:
        v_ref[...] = p
    p = group(3)
    for c in range(n_chunks):
        qc = _pair_rotate(chunk(p, c), ret_tab(ct_ref, c), ret_tab(st_ref, c), 1)
        qr_ref[:, c * LANES:(c + 1) * LANES] = qc.astype(qr_ref.dtype)
    p = group(4)
    for c in range(n_chunks):
        kc = _pair_rotate(chunk(p, c), ret_tab(ct_ref, c), ret_tab(st_ref, c), 1) * k_scale
        if transposed:
            krt_ref[c * LANES:(c + 1) * LANES, :] = kc.T.astype(BF16)
        else:
            kr_ref[:, c * LANES:(c + 1) * LANES] = kc
    for gi, ref in ((5, vr_ref), (6, gr_ref), (7, ga_ref), (8, gb_ref)):
        ref[...] = group(gi).astype(ref.dtype)


def _proj(x, mod, mod_kind, rows_per_mod, norm_g, w_in, tables, n_tab_tiles, transposed, batch, seq):
    rows = x.shape[0]
    tm = min(PROJ_TM, rows)
    row = lambda i: (i, 0)
    const = lambda i: (0, 0)
    tab = lambda w: pl.BlockSpec((tm, w), lambda i: (i % n_tab_tiles, 0))
    rowspec = pl.BlockSpec((tm, D_MODEL), row)
    if transposed:
        tiles = seq // tm
        tspec = pl.BlockSpec((None, D_MODEL, tm), lambda i: (i // tiles, 0, i % tiles))
        t_shape = lambda dt: jax.ShapeDtypeStruct((batch, D_MODEL, seq), dt)
        r_shape = lambda dt: jax.ShapeDtypeStruct((rows, D_MODEL), dt)
        vspec = pl.BlockSpec((tm, ATTN_V_WIDTH), row)
        v_shape = jax.ShapeDtypeStruct((rows, ATTN_V_WIDTH), BF16)
        out_specs = [rowspec, tspec, tspec, rowspec, vspec, rowspec, tspec, rowspec, rowspec, rowspec, rowspec]
        out_shape = [r_shape(BF16), t_shape(F32), t_shape(BF16), r_shape(F32), v_shape, r_shape(BF16),
                     t_shape(BF16), r_shape(BF16), r_shape(BF16), r_shape(BF16), r_shape(BF16)]
    else:
        out_specs = [rowspec] * N_PROJ
        out_shape = [jax.ShapeDtypeStruct((rows, D_MODEL), F32)] * N_PROJ
    return pl.pallas_call(
        functools.partial(_proj_kernel, transposed=transposed),
        grid=(rows // tm,),
        in_specs=[
            rowspec,
            _mod_spec(mod_kind, tm, rows_per_mod, 3),
            _mod_spec(mod_kind, tm, rows_per_mod, 4),
            pl.BlockSpec((1, D_MODEL), const),
            pl.BlockSpec((D_MODEL, N_PROJ * D_MODEL), const, pipeline_mode=pl.Buffered(1)),
            tab(LANES), tab(LANES), tab(RET_DK), tab(RET_DK),
        ],
        out_specs=out_specs,
        out_shape=out_shape,
        compiler_params=_cparams(("parallel",)),
        name="mixer_proj",
    )(x, mod, mod, norm_g, w_in, *tables)


def _lambda(lq1, lk1, lq2, lk2, lam_init):
    return (jnp.exp(jnp.sum(lq1 * lk1, axis=-1, keepdims=True))
            - jnp.exp(jnp.sum(lq2 * lk2, axis=-1, keepdims=True)) + lam_init)


def _attn_kernel(qi_ref, kj_ref, q_ref, kt_ref, v_ref, lq1_ref, lk1_ref, lq2_ref, lk2_ref, sg_ref, o_ref,
                 qz_ref, m_ref, acc_ref, *, tq, lam_init):
    step = pl.program_id(1)
    qi = qi_ref[step]
    kj = kj_ref[step]
    lane = lax.broadcasted_iota(jnp.int32, (tq, DA_DV), 1)
    n_rep = tq // LANES

    @pl.when(kj == 0)
    def _():
        for h in range(DA_HEADS):
            qh = q_ref[:, h * DA_DV:(h + 1) * DA_DV]
            qz_ref[h, :tq, :] = jnp.where(lane < DA_DK, qh, jnp.zeros_like(qh))
            qz_ref[h, tq:, :] = jnp.where(lane >= DA_DK, qh, jnp.zeros_like(qh))

    def block(masked, first):
        if masked:
            r = lax.broadcasted_iota(jnp.int32, (2 * tq, tq), 0) % tq
            c = lax.broadcasted_iota(jnp.int32, (2 * tq, tq), 1)
            keep = c <= r
        for h in range(DA_HEADS):
            s = jnp.dot(qz_ref[h], kt_ref[h * DA_DV:(h + 1) * DA_DV, :], preferred_element_type=F32)
            if masked:
                s = jnp.where(keep, s, NEG_INF)
            row_max = jnp.max(s, axis=-1, keepdims=True)
            if first:
                m_new = jnp.broadcast_to(row_max, (2 * tq, LANES))
            else:
                m_old = m_ref[h]
                m_new = jnp.maximum(m_old, row_max)
            p = jnp.exp2(s - jnp.concatenate([m_new] * n_rep, axis=1))
            pv = jnp.dot(p.astype(BF16), v_ref[:, h * 2 * DA_DV:(h + 1) * 2 * DA_DV], preferred_element_type=F32)
            if first:
                acc_ref[h] = pv
            else:
                alpha = jnp.exp2(m_old - m_new)
                acc_ref[h] = jnp.concatenate([alpha, alpha], axis=1) * acc_ref[h] + pv
            m_ref[h] = m_new

    for masked in (False, True):
        for first in (False, True):
            @pl.when(((kj == qi) if masked else (kj < qi)) & ((kj == 0) if first else (kj > 0)))
            def _(masked=masked, first=first):
                block(masked, first)

    @pl.when(kj == qi)
    def _():
        lam = _lambda(lq1_ref[...], lk1_ref[...], lq2_ref[...], lk2_ref[...], lam_init)
        for h in range(DA_HEADS):
            o = acc_ref[h, :, :DA_DV] / acc_ref[h, :, DA_DV:]
            d = o[:tq] - lam * o[tq:]
            o_ref[:, h * DA_DV:(h + 1) * DA_DV] = (_rms(d, sg_ref[...], SUBLN_EPS) * (1.0 - lam_init)).astype(o_ref.dtype)


def _attn_prompt(q, kt, v, lam_vecs, subln_g, batch, seq, lam_init):
    tq = min(ATTN_TQ, seq)
    nq = seq // tq
    pairs = [(i, j) for i in range(nq) for j in range(i + 1)]
    qi = jnp.asarray([p[0] for p in pairs], jnp.int32)
    kj = jnp.asarray([p[1] for p in pairs], jnp.int32)
    vec = lambda w: pl.BlockSpec((1, w), lambda b, s, qi, kj: (0, 0))
    grid_spec = pltpu.PrefetchScalarGridSpec(
        num_scalar_prefetch=2,
        grid=(batch, len(pairs)),
        in_specs=[
            pl.BlockSpec((tq, D_MODEL), lambda b, s, qi, kj: (b * nq + qi[s], 0)),
            pl.BlockSpec((None, D_MODEL, tq), lambda b, s, qi, kj: (b, 0, kj[s])),
            pl.BlockSpec((tq, ATTN_V_WIDTH), lambda b, s, qi, kj: (b * nq + kj[s], 0)),
            vec(DA_DK), vec(DA_DK), vec(DA_DK), vec(DA_DK), vec(DA_DV),
        ],
        out_specs=pl.BlockSpec((tq, D_MODEL), lambda b, s, qi, kj: (b * nq + qi[s], 0)),
        scratch_shapes=[
            pltpu.VMEM((DA_HEADS, 2 * tq, DA_DV), BF16),
            pltpu.VMEM((DA_HEADS, 2 * tq, LANES), F32),
            pltpu.VMEM((DA_HEADS, 2 * tq, 2 * DA_DV), F32),
        ],
    )
    return pl.pallas_call(
        functools.partial(_attn_kernel, tq=tq, lam_init=lam_init),
        grid_spec=grid_spec,
        out_shape=jax.ShapeDtypeStruct((batch * seq, D_MODEL), BF16),
        compiler_params=_cparams(("parallel", "arbitrary")),
        name="diff_attn_prompt",
    )(qi, kj, q, kt, v, *lam_vecs, subln_g)


def _attn_sample_kernel(pt_ref, *refs, n_pages, page, dec_seq, lam_init):
    k_refs = refs[:n_pages]
    v_refs = refs[n_pages:2 * n_pages]
    (q_ref, kn_ref, vn_ref, lq1_ref, lk1_ref, lq2_ref, lk2_ref, sg_ref, o_ref, s_ref) = refs[2 * n_pages:]
    del pt_ref
    rows = 2 * dec_seq
    q = q_ref[...].astype(BF16)
    for i in range(n_pages):
        s_ref[:, :, i * page:(i + 1) * page] = jnp.einsum(
            "hrk,hkt->hrt", q, k_refs[i][...].astype(BF16), preferred_element_type=F32)
    qf = q.astype(F32)
    kn = kn_ref[...].astype(BF16).astype(F32)
    vn = vn_ref[...].astype(BF16).astype(F32)
    r = lax.broadcasted_iota(jnp.int32, (DA_HEADS, rows, 1), 1) % dec_seq
    s_new = [jnp.where(r >= t, jnp.sum(qf * kn[:, t:t + 1, :], axis=-1, keepdims=True), NEG_INF)
             for t in range(dec_seq)]
    s_old = s_ref[...]
    m = jnp.max(s_old, axis=-1, keepdims=True)
    for s_t in s_new:
        m = jnp.maximum(m, s_t)
    p_old = jnp.exp2(s_old - m)
    denom = jnp.sum(p_old, axis=-1, keepdims=True)
    acc_new = jnp.zeros((DA_HEADS, rows, DA_DV), F32)
    for t, s_t in enumerate(s_new):
        p_t = jnp.exp2(s_t - m)
        denom = denom + p_t
        acc_new = acc_new + p_t.astype(BF16).astype(F32) * vn[:, t:t + 1, :]
    s_ref[...] = p_old
    lam = _lambda(lq1_ref[...], lk1_ref[...], lq2_ref[...], lk2_ref[...], lam_init)
    for h in range(DA_HEADS):
        acc = acc_new[h]
        for i in range(n_pages):
            acc = acc + jnp.dot(s_ref[h, :, i * page:(i + 1) * page].astype(BF16),
                                v_refs[i][pl.ds(h, page, stride=DA_HEADS), :].astype(BF16),
                                preferred_element_type=F32)
        o = acc / denom[h]
        d = o[:dec_seq] - lam * o[dec_seq:]
        o_ref[:, h * DA_DV:(h + 1) * DA_DV] = _rms(d, sg_ref[...], SUBLN_EPS) * (1.0 - lam_init)


def _attn_sample(page_table, cache_kt, cache_v, q_bd, k_new, v_new, lam_vecs, subln_g, lam_init):
    dec_batch, n_pages = page_table.shape
    page = cache_v.shape[1]
    dec_seq = k_new.shape[2]
    kspec = lambda i: pl.BlockSpec((None, DA_HEADS, DA_DV, page), lambda b, pt: (pt[b, i], 0, 0, 0))
    cache_v = cache_v.reshape(cache_v.shape[0], page * DA_HEADS, DA_DV)
    vspec = lambda i: pl.BlockSpec((None, page * DA_HEADS, DA_DV), lambda b, pt: (pt[b, i], 0, 0))
    per_b = lambda r: pl.BlockSpec((None, DA_HEADS, r, DA_DV), lambda b, pt: (b, 0, 0, 0))
    vec = lambda w: pl.BlockSpec((1, w), lambda b, pt: (0, 0))
    grid_spec = pltpu.PrefetchScalarGridSpec(
        num_scalar_prefetch=1,
        grid=(dec_batch,),
        in_specs=[kspec(i) for i in range(n_pages)] + [vspec(i) for i in range(n_pages)] + [
            per_b(2 * dec_seq), per_b(dec_seq), per_b(dec_seq),
            vec(DA_DK), vec(DA_DK), vec(DA_DK), vec(DA_DK), vec(DA_DV)],
        out_specs=pl.BlockSpec((None, dec_seq, D_MODEL), lambda b, pt: (b, 0, 0)),
        scratch_shapes=[pltpu.VMEM((DA_HEADS, 2 * dec_seq, n_pages * page), F32)],
    )
    return pl.pallas_call(
        functools.partial(_attn_sample_kernel, n_pages=n_pages, page=page, dec_seq=dec_seq, lam_init=lam_init),
        grid_spec=grid_spec,
        out_shape=jax.ShapeDtypeStruct((dec_batch, dec_seq, D_MODEL), F32),
        compiler_params=_cparams(("parallel",)),
        name="diff_attn_sample",
    )(page_table, *([cache_kt] * n_pages), *([cache_v] * n_pages), q_bd, k_new, v_new, *lam_vecs, subln_g)


def _group_norm(o, g):
    mu = jnp.mean(o, axis=-1, keepdims=True)
    d = o - mu
    return d * lax.rsqrt(jnp.mean(d * d, axis=-1, keepdims=True) + GN_EPS) * g


def _ret_prompt_kernel(lg_ref, q_ref, kt_ref, v_ref, g_ref, o_ref, st_ref, decay_ref, *, chunk):
    c = pl.program_id(0)
    batch = q_ref.shape[0]

    @pl.when(c == 0)
    def _():
        st_ref[...] = jnp.zeros(st_ref.shape, F32)
        ri = lax.broadcasted_iota(jnp.int32, (chunk, chunk), 0)
        ci = lax.broadcasted_iota(jnp.int32, (chunk, chunk), 1)
        dist = (ri - ci).astype(F32)
        for h in range(RET_HEADS):
            decay_ref[h] = jnp.where(dist >= 0, jnp.exp(jnp.maximum(dist, 0.0) * lg_ref[h]), 0.0)

    row = lax.broadcasted_iota(jnp.int32, (chunk, 1), 0)
    col = lax.broadcasted_iota(jnp.int32, (1, chunk), 1)
    for h in range(RET_HEADS):
        lg = lg_ref[h]
        sl = slice(h * RET_DK, (h + 1) * RET_DK)
        q_decay = jnp.exp((row + 1).astype(F32) * lg)
        k_decay = jnp.exp((chunk - 1 - col).astype(F32) * lg)
        carry = jnp.exp(jnp.full((1, 1), chunk, F32) * lg)
        for b in range(batch):
            q = q_ref[b, :, sl]
            kt = kt_ref[b, sl, :]
            v = v_ref[b, :, sl]
            scores = jnp.dot(q, kt, preferred_element_type=F32) * decay_ref[h]
            inner = jnp.dot(scores.astype(BF16), v, preferred_element_type=F32)
            state = st_ref[b, h]
            cross = jnp.dot(q, state.astype(BF16), preferred_element_type=F32) * q_decay
            o_ref[b, :, sl] = _group_norm(inner + cross, g_ref[:, sl]).astype(o_ref.dtype)
            ktd = (kt.astype(F32) * k_decay).astype(BF16)
            st_ref[b, h] = carry * state + jnp.dot(ktd, v, preferred_element_type=F32)


def _ret_prompt(log_g, q, kt, v, gn_g, batch, seq):
    chunk = min(RET_L, seq)
    tok = pl.BlockSpec((batch, chunk, D_MODEL), lambda c: (0, c, 0))
    o_r, state = pl.pallas_call(
        functools.partial(_ret_prompt_kernel, chunk=chunk),
        grid=(seq // chunk,),
        in_specs=[
            pl.BlockSpec(memory_space=pltpu.SMEM),
            tok,
            pl.BlockSpec((batch, D_MODEL, chunk), lambda c: (0, 0, c)),
            tok,
            pl.BlockSpec((1, D_MODEL), lambda c: (0, 0)),
        ],
        out_specs=[tok, pl.BlockSpec((batch, RET_HEADS, RET_DK, RET_DV), lambda c: (0, 0, 0, 0))],
        out_shape=[jax.ShapeDtypeStruct((batch, seq, D_MODEL), BF16),
                   jax.ShapeDtypeStruct((batch, RET_HEADS, RET_DK, RET_DV), F32)],
        scratch_shapes=[pltpu.VMEM((RET_HEADS, chunk, chunk), F32)],
        compiler_params=_cparams(("arbitrary",)),
        name="retention_prompt",
    )(log_g, q.reshape(batch, seq, D_MODEL), kt, v.reshape(batch, seq, D_MODEL), gn_g)
    return o_r.reshape(batch * seq, D_MODEL), state


def _ret_sample_kernel(lg_ref, q_ref, k_ref, v_ref, s_ref, g_ref, o_ref, st_ref, *, dec_seq):
    n_b, rows = q_ref.shape[:2]
    ri = lax.broadcasted_iota(jnp.int32, (rows, 1), 0)
    zpad = jnp.zeros((LANES - rows, RET_DK), F32)
    for h in range(RET_HEADS):
        lg = lg_ref[h]
        sl = slice(h * RET_DK, (h + 1) * RET_DK)
        decay = []
        for j in range(dec_seq):
            dist = (ri - j).astype(F32)
            decay.append(jnp.where(dist >= 0, jnp.exp(jnp.maximum(dist, 0.0) * lg), 0.0))
        q_decay = jnp.exp((ri + 1).astype(F32) * lg)
        k_decay = jnp.where(ri < dec_seq, jnp.exp((dec_seq - 1 - ri).astype(F32) * lg), 0.0)
        carry = jnp.exp(jnp.full((1, 1), dec_seq, F32) * lg)
        for b in range(n_b):
            q = q_ref[b, :, sl]
            k = k_ref[b, :, sl]
            v = v_ref[b, :, sl]
            qb = q.astype(BF16).astype(F32)
            kb = k.astype(BF16).astype(F32)
            vb = v.astype(BF16).astype(F32)
            state = s_ref[b, h]
            inner = jnp.zeros((rows, RET_DV), F32)
            for j in range(dec_seq):
                score = jnp.sum(qb * kb[j:j + 1, :], axis=-1, keepdims=True) * decay[j]
                inner = inner + score.astype(BF16).astype(F32) * vb[j:j + 1, :]
            cross = jnp.dot(q.astype(BF16), state.astype(BF16), preferred_element_type=F32) * q_decay
            o_ref[b, :, sl] = _group_norm(inner + cross, g_ref[:, sl])
            kd_t = jnp.concatenate([k * k_decay, zpad], axis=0).T.astype(BF16)
            v_pad = jnp.concatenate([v, zpad], axis=0).astype(BF16)
            st_ref[b, h] = carry * state + jnp.dot(kd_t, v_pad, preferred_element_type=F32)


def _ret_sample(log_g, q, k, v, state, gn_g, dec_seq):
    dec_batch, rows = q.shape[:2]
    n_b = math.gcd(dec_batch, RET_SAMPLE_NB)
    tok = pl.BlockSpec((n_b, rows, D_MODEL), lambda b: (b, 0, 0))
    st = pl.BlockSpec((n_b, RET_HEADS, RET_DK, RET_DV), lambda b: (b, 0, 0, 0))
    return pl.pallas_call(
        functools.partial(_ret_sample_kernel, dec_seq=dec_seq),
        grid=(dec_batch // n_b,),
        in_specs=[pl.BlockSpec(memory_space=pltpu.SMEM), tok, tok, tok, st,
                  pl.BlockSpec((1, D_MODEL), lambda b: (0, 0))],
        out_specs=[tok, st],
        out_shape=[jax.ShapeDtypeStruct((dec_batch, rows, D_MODEL), F32),
                   jax.ShapeDtypeStruct(state.shape, F32)],
        compiler_params=_cparams(("parallel",)),
        name="retention_sample",
    )(log_g, q, k, v, state, gn_g)


def kernel(x_prompt, x_sample, cache_k, cache_v, state_ret, page_table, c_prompt, c_sample, ada_w, ada_b, norm_ffn1, norm_mix, norm_ffn2, ffn1_w_in, ffn1_w_out, ffn2_w_in, ffn2_w_out, w_in, w_out, lam_q1, lam_k1, lam_q2, lam_k2, subln_g, ret_norm_g, norm_final):
    batch, seq, _ = x_prompt.shape
    dec_batch, dec_seq, _ = x_sample.shape
    depth = ada_w.shape[0]
    page = cache_k.shape[2]
    past_len = page_table.shape[1] * page
    log_g = jnp.log1p(-jnp.exp2(-5.0 - jnp.arange(RET_HEADS, dtype=F32)))
    nf = norm_final[None, :]

    tab_p = _rotation_tables(seq, 0)
    sub = 8
    tab_s = [jnp.tile(t[:dec_seq], (dec_batch, 1)) for t in _rotation_tables(sub, past_len)]

    rows_s = dec_batch * dec_seq
    n_c = rows_s + batch
    c_all = jnp.concatenate([jnp.repeat(c_sample, dec_seq, axis=0), c_prompt,
                             jnp.zeros((-n_c % sub, D_MODEL), F32)], axis=0)

    yp = x_prompt.reshape(batch * seq, D_MODEL)
    ys = x_sample.reshape(dec_batch * dec_seq, D_MODEL)
    kp_l, vp_l, sp_l, ks_l, vs_l, ss_l = [], [], [], [], [], []
    for l in range(depth):
        lam_init = 0.8 - 0.6 * math.exp(-0.3 * l)
        mod_s = _adaln(c_all, ada_w[l], ada_b[l][None, :])
        mod_p = mod_s[rows_s:n_c].reshape(batch, 1, N_MOD * D_MODEL)
        w1i, w1o = ffn1_w_in[l].astype(BF16), ffn1_w_out[l].astype(BF16)
        w2i, w2o = ffn2_w_in[l].astype(BF16), ffn2_w_out[l].astype(BF16)
        wi, wo = w_in[l].astype(BF16), w_out[l].astype(BF16)
        g1, gm, g2 = norm_ffn1[l][None, :], norm_mix[l][None, :], norm_ffn2[l][None, :]
        lam_vecs = (lam_q1[l][None, :], lam_k1[l][None, :], lam_q2[l][None, :], lam_k2[l][None, :])
        sg = subln_g[l][None, :]
        gn = ret_norm_g[l][None, :]
        last = l == depth - 1

        x1 = _ffn(yp, mod_p, "batch", seq, (0, 1, 2), g1, w1i, w1o, nf, False)
        (q, ktf, ktb, vf, vb, qr, krt, vr, gr, ga, gb) = _proj(
            x1, mod_p, "batch", seq, gm, wi, tab_p, seq // min(PROJ_TM, seq), True, batch, seq)
        o_a = _attn_prompt(q, ktb, vb, lam_vecs, sg, batch, seq, lam_init)
        o_r, sp = _ret_prompt(log_g, qr, krt, vr, gn, batch, seq)
        yp = _ffn(x1, mod_p, "batch", seq, (6, 7, 8), g2, w2i, w2o, nf, last, mixer=(5, o_a, o_r, gr, ga, gb, wo))
        kp_l.append(jnp.transpose(ktf.reshape(batch, 2 * DA_HEADS, DA_DK, seq), (0, 3, 1, 2)))
        vp_l.append(vf.reshape(batch, seq, DA_HEADS, DA_DV))
        sp_l.append(sp)

        x1 = _ffn(ys, mod_s, "token", 1, (0, 1, 2), g1, w1i, w1o, nf, False)
        (q, k, v, qr, kr, vr, gr, ga, gb) = _proj(
            x1, mod_s, "token", 1, gm, wi, tab_s, rows_s // min(PROJ_TM, rows_s), False, dec_batch, dec_seq)
        qh = q.reshape(dec_batch, dec_seq, DA_HEADS, 2, DA_DK).transpose(0, 2, 3, 1, 4)
        eye = jnp.eye(2, dtype=F32)
        q_bd = (qh[:, :, :, :, None, :] * eye[None, None, :, None, :, None]).reshape(
            dec_batch, DA_HEADS, 2 * dec_seq, DA_DV)
        to_heads = lambda a: a.reshape(dec_batch, dec_seq, DA_HEADS, DA_DV).transpose(0, 2, 1, 3)
        cache_kt = jnp.transpose(cache_k[l], (0, 2, 3, 1)).reshape(-1, DA_HEADS, DA_DV, page)
        o_a = _attn_sample(page_table, cache_kt, cache_v[l], q_bd, to_heads(k), to_heads(v), lam_vecs, sg, lam_init)
        pad = lambda a: jnp.pad(a.reshape(dec_batch, dec_seq, D_MODEL), ((0, 0), (0, sub - dec_seq), (0, 0)))
        o_r, ssm = _ret_sample(log_g, pad(qr), pad(kr), pad(vr), state_ret[l], gn, dec_seq)
        o_r = o_r[:, :dec_seq].reshape(rows_s, D_MODEL)
        ys = _ffn(x1, mod_s, "token", 1, (6, 7, 8), g2, w2i, w2o, nf, last,
                  mixer=(5, o_a.reshape(rows_s, D_MODEL), o_r, gr, ga, gb, wo))
        ks_l.append(k.reshape(dec_batch, dec_seq, 2 * DA_HEADS, DA_DK))
        vs_l.append(v.reshape(dec_batch, dec_seq, DA_HEADS, DA_DV))
        ss_l.append(ssm)

    return (yp.reshape(batch, seq, D_MODEL), ys.reshape(dec_batch, dec_seq, D_MODEL),
            jnp.stack(kp_l), jnp.stack(vp_l), jnp.stack(sp_l), jnp.stack(ks_l), jnp.stack(vs_l), jnp.stack(ss_l))
```

```python
import functools
import math

import jax
import jax.numpy as jnp
from jax import lax
from jax.experimental import pallas as pl
from jax.experimental.pallas import tpu as pltpu

F32 = jnp.float32
BF16 = jnp.bfloat16

D_MODEL = 1024
DA_DK = 64
DA_DV = 128
DA_HEADS = 8
RET_HEADS = 4
RET_DK = 256
RET_DV = 256
D_FF = 2816
N_MOD = 9
N_PROJ = 9
HALF_STEP = 0.5
ROPE_THETA = 10000.0
RET_THETA = 10000.0
NORM_EPS = 1e-6
SUBLN_EPS = 1e-5
GN_EPS = 1e-5
NEG_INF = -1e30
LOG2E = 1.4426950408889634
LANES = 128
VMEM_LIMIT = 56 * 1024 * 1024

FFN_TM = 512
FFN_TF = 1408
PROJ_TM = 256
ATTN_TQ = 512
ATTN_V_WIDTH = 2 * DA_HEADS * DA_DV
RET_L = 512
RET_SAMPLE_NB = 8


def _cparams(sem):
    return pltpu.CompilerParams(dimension_semantics=sem, vmem_limit_bytes=VMEM_LIMIT)


def _rms(x, g, eps):
    return x * lax.rsqrt(jnp.mean(x * x, axis=-1, keepdims=True) + eps) * g


def _mod_spec(kind, tm, rows_per_mod, piece):
    if kind == "batch":
        return pl.BlockSpec((None, 1, D_MODEL), lambda i, *_: ((i * tm) // rows_per_mod, 0, piece))
    return pl.BlockSpec((tm, D_MODEL), lambda i, *_: (i, piece))


def _adaln_kernel(c_ref, w_ref, b_ref, o_ref):
    c = c_ref[...]
    a = (c * jax.nn.sigmoid(c)).astype(BF16)
    o_ref[...] = jnp.dot(a, w_ref[...].astype(BF16), preferred_element_type=F32) + b_ref[...]


def _adaln(c, w, b):
    m = c.shape[0]
    n = w.shape[1]
    tn = 1152
    return pl.pallas_call(
        _adaln_kernel,
        grid=(n // tn,),
        in_specs=[
            pl.BlockSpec((m, D_MODEL), lambda j: (0, 0)),
            pl.BlockSpec((D_MODEL, tn), lambda j: (0, j)),
            pl.BlockSpec((1, tn), lambda j: (0, j)),
        ],
        out_specs=pl.BlockSpec((m, tn), lambda j: (0, j)),
        out_shape=jax.ShapeDtypeStruct((m, n), F32),
        compiler_params=_cparams(("arbitrary",)),
        name="adaln",
    )(c, w, b)


def _merged_mixer(gtm_ref, oa_ref, or_ref, gr_ref, ga_ref, gb_ref, wm_ref):
    o_r = or_ref[...].astype(F32) * gr_ref[...].astype(F32)
    merged = ga_ref[...].astype(F32) * oa_ref[...].astype(F32) + gb_ref[...].astype(F32) * o_r
    return gtm_ref[...] * jnp.dot(merged.astype(BF16), wm_ref[...], preferred_element_type=F32)


def _ffn_kernel(*refs, final_norm, merge):
    if merge:
        (x_ref, gtm_ref, oa_ref, or_ref, gr_ref, ga_ref, gb_ref, wm_ref), refs = refs[:8], refs[8:]
    else:
        x_ref, refs = refs[0], refs[1:]
    sh_ref, sc_ref, gt_ref, g_ref, wi_ref, wo_ref, nf_ref, o_ref = refs
    x = x_ref[...]
    if merge:
        x = x + _merged_mixer(gtm_ref, oa_ref, or_ref, gr_ref, ga_ref, gb_ref, wm_ref)
    h = (_rms(x, g_ref[...], NORM_EPS) * (1.0 + sc_ref[...]) + sh_ref[...]).astype(BF16)
    acc = None
    for c in range(D_FF // FFN_TF):
        lo = c * FFN_TF
        a = jnp.dot(h, wi_ref[:, lo:lo + FFN_TF], preferred_element_type=F32)
        b = jnp.dot(h, wi_ref[:, D_FF + lo:D_FF + lo + FFN_TF], preferred_element_type=F32)
        act = (a * jax.nn.sigmoid(a) * b).astype(BF16)
        part = jnp.dot(act, wo_ref[lo:lo + FFN_TF, :], preferred_element_type=F32)
        acc = part if acc is None else acc + part
    out = x + HALF_STEP * gt_ref[...] * acc
    if final_norm:
        out = _rms(out, nf_ref[...], NORM_EPS)
    o_ref[...] = out


def _ffn(x, mod, mod_kind, rows_per_mod, pieces, norm_g, w_in, w_out, norm_final, final_norm, mixer=None):
    rows = x.shape[0]
    tm = min(FFN_TM, rows)
    if mixer is not None and mixer[1].dtype == F32:
        tm = tm // 2
    const = lambda i: (0, 0)
    rowspec = pl.BlockSpec((tm, D_MODEL), lambda i: (i, 0))
    resident = lambda shape: pl.BlockSpec(shape, const, pipeline_mode=pl.Buffered(1))
    in_specs = [rowspec]
    args = [x]
    if mixer is not None:
        in_specs += [_mod_spec(mod_kind, tm, rows_per_mod, mixer[0])] + [rowspec] * 5 + [
            resident((D_MODEL, D_MODEL))]
        args += [mod, *mixer[1:]]
    in_specs += [
        _mod_spec(mod_kind, tm, rows_per_mod, pieces[0]),
        _mod_spec(mod_kind, tm, rows_per_mod, pieces[1]),
        _mod_spec(mod_kind, tm, rows_per_mod, pieces[2]),
        pl.BlockSpec((1, D_MODEL), const),
        resident((D_MODEL, 2 * D_FF)),
        resident((D_FF, D_MODEL)),
        pl.BlockSpec((1, D_MODEL), const),
    ]
    args += [mod, mod, mod, norm_g, w_in, w_out, norm_final]
    return pl.pallas_call(
        functools.partial(_ffn_kernel, final_norm=final_norm, merge=mixer is not None),
        grid=(rows // tm,),
        in_specs=in_specs,
        out_specs=rowspec,
        out_shape=jax.ShapeDtypeStruct((rows, D_MODEL), F32),
        compiler_params=_cparams(("parallel",)),
        name="ffn",
    )(*args)


def _table_kernel(inv_r_ref, sgn_r_ref, inv_t_ref, sgn_t_ref, cr_ref, sr_ref, ct_ref, st_ref,
                  lcr_ref, lsr_ref, lct_ref, lst_ref, *, offset):
    rows = cr_ref.shape[0]
    i = pl.program_id(0)

    @pl.when(i == 0)
    def _():
        local = lax.broadcasted_iota(jnp.int32, (rows, 1), 0).astype(F32)
        ang = local * inv_r_ref[...]
        lcr_ref[...] = jnp.cos(ang)
        lsr_ref[...] = jnp.sin(ang)
        ang = local * inv_t_ref[...]
        lct_ref[...] = jnp.cos(ang)
        lst_ref[...] = jnp.sin(ang)

    base = jnp.full((1, 1), i * rows + offset, jnp.int32).astype(F32)
    for inv_ref, sgn_ref, lc_ref, ls_ref, c_ref, s_ref in (
            (inv_r_ref, sgn_r_ref, lcr_ref, lsr_ref, cr_ref, sr_ref),
            (inv_t_ref, sgn_t_ref, lct_ref, lst_ref, ct_ref, st_ref)):
        ang = base * inv_ref[...]
        cb, sb = jnp.cos(ang), jnp.sin(ang)
        lc, ls = lc_ref[...], ls_ref[...]
        c_ref[...] = lc * cb - ls * sb
        s_ref[...] = (ls * cb + lc * sb) * sgn_ref[...]


def _rotation_tables(n_pos, offset):
    lane = jnp.arange(LANES)
    half = DA_DK // 2
    inv_r = (ROPE_THETA ** (-((lane % DA_DK) % half).astype(F32) / half))[None, :]
    sgn_r = jnp.where((lane % DA_DK) < half, -1.0, 1.0).astype(F32)[None, :]
    lane_t = jnp.arange(RET_DK)
    angle = 1.0 / (RET_THETA ** jnp.linspace(0.0, 1.0, RET_DK // 2, dtype=F32))
    inv_t = angle[lane_t // 2][None, :]
    sgn_t = jnp.where(lane_t % 2 == 0, -1.0, 1.0).astype(F32)[None, :]
    rows = min(n_pos, 512)
    vec = lambda w: pl.BlockSpec((1, w), lambda i: (0, 0))
    tab = lambda w: pl.BlockSpec((rows, w), lambda i: (i, 0))
    return pl.pallas_call(
        functools.partial(_table_kernel, offset=offset),
        grid=(n_pos // rows,),
        in_specs=[vec(LANES), vec(LANES), vec(RET_DK), vec(RET_DK)],
        out_specs=[tab(LANES), tab(LANES), tab(RET_DK), tab(RET_DK)],
        out_shape=[jax.ShapeDtypeStruct((n_pos, LANES), F32), jax.ShapeDtypeStruct((n_pos, LANES), F32),
                   jax.ShapeDtypeStruct((n_pos, RET_DK), F32), jax.ShapeDtypeStruct((n_pos, RET_DK), F32)],
        scratch_shapes=[pltpu.VMEM((rows, LANES), F32), pltpu.VMEM((rows, LANES), F32),
                        pltpu.VMEM((rows, RET_DK), F32), pltpu.VMEM((rows, RET_DK), F32)],
        compiler_params=_cparams(("arbitrary",)),
        name="rotation_tables",
    )(inv_r, sgn_r, inv_t, sgn_t)


def _pair_rotate(x, cos, sin_signed, shift):
    lane = lax.broadcasted_iota(jnp.int32, x.shape, 1)
    partner = jnp.where((lane % (2 * shift)) < shift, pltpu.roll(x, LANES - shift, 1), pltpu.roll(x, shift, 1))
    return x * cos + partner * sin_signed


def _proj_kernel(x_ref, sh_ref, sc_ref, g_ref, w_ref, cr_ref, sr_ref, ct_ref, st_ref, *out_refs, transposed):
    h = (_rms(x_ref[...], g_ref[...], NORM_EPS) * (1.0 + sc_ref[...]) + sh_ref[...]).astype(BF16)
    cr, sr = cr_ref[...], sr_ref[...]
    n_chunks = D_MODEL // LANES
    q_scale = (DA_DK ** -0.5) * LOG2E
    k_scale = RET_DK ** -0.5

    def group(gi):
        return jnp.dot(h, w_ref[:, gi * D_MODEL:(gi + 1) * D_MODEL], preferred_element_type=F32)

    def chunk(p, c):
        return p[:, c * LANES:(c + 1) * LANES]

    def ret_tab(t_ref, c):
        half = (c % (RET_DK // LANES)) * LANES
        return t_ref[:, half:half + LANES]

    if transposed:
        (q_ref, ktf_ref, ktb_ref, vf_ref, vb_ref, qr_ref, krt_ref, vr_ref, gr_ref, ga_ref, gb_ref) = out_refs
    else:
        (q_ref, k_ref, v_ref, qr_ref, kr_ref, vr_ref, gr_ref, ga_ref, gb_ref) = out_refs

    p = group(0)
    for c in range(n_chunks):
        q_ref[:, c * LANES:(c + 1) * LANES] = (_pair_rotate(chunk(p, c), cr, sr, DA_DK // 2) * q_scale).astype(q_ref.dtype)
    p = group(1)
    for c in range(n_chunks):
        kc = _pair_rotate(chunk(p, c), cr, sr, DA_DK // 2)
        if transposed:
            kt = kc.T
            ktf_ref[c * LANES:(c + 1) * LANES, :] = kt
            ktb_ref[c * LANES:(c + 1) * LANES, :] = kt.astype(BF16)
        else:
            k_ref[:, c * LANES:(c + 1) * LANES] = kc
    p = group(2)
    if transposed:
        vf_ref[...] = p
        ones = jnp.ones((p.shape[0], DA_DV), BF16)
        for hd in range(DA_HEADS):
            vb_ref[:, 2 * hd * DA_DV:(2 * hd + 1) * DA_DV] = chunk(p, hd).astype(BF16)
            vb_ref[:, (2 * hd + 1) * DA_DV:(2 * hd + 2) * DA_DV] = ones
    else:
        v_ref[...] = p
    p = group(3)
    for c in range(n_chunks):
        qc = _pair_rotate(chunk(p, c), ret_tab(ct_ref, c), ret_tab(st_ref, c), 1)
        qr_ref[:, c * LANES:(c + 1) * LANES] = qc.astype(qr_ref.dtype)
    p = group(4)
    for c in range(n_chunks):
        kc = _pair_rotate(chunk(p, c), ret_tab(ct_ref, c), ret_tab(st_ref, c), 1) * k_scale
        if transposed:
            krt_ref[c * LANES:(c + 1) * LANES, :] = kc.T.astype(BF16)
        else:
            kr_ref[:, c * LANES:(c + 1) * LANES] = kc
    vr_ref[...] = group(5).astype(vr_ref.dtype)
    p = group(6)
    gr_ref[...] = (p * jax.nn.sigmoid(p)).astype(gr_ref.dtype)
    ga_ref[...] = jax.nn.sigmoid(group(7)).astype(ga_ref.dtype)
    gb_ref[...] = jax.nn.sigmoid(group(8)).astype(gb_ref.dtype)


def _proj(x, mod, mod_kind, rows_per_mod, norm_g, w_in, tables, n_tab_tiles, transposed, batch, seq):
    rows = x.shape[0]
    tm = min(PROJ_TM, rows)
    row = lambda i: (i, 0)
    const = lambda i: (0, 0)
    tab = lambda w: pl.BlockSpec((tm, w), lambda i: (i % n_tab_tiles, 0))
    rowspec = pl.BlockSpec((tm, D_MODEL), row)
    if transposed:
        tiles = seq // tm
        tspec = pl.BlockSpec((None, D_MODEL, tm), lambda i: (i // tiles, 0, i % tiles))
        t_shape = lambda dt: jax.ShapeDtypeStruct((batch, D_MODEL, seq), dt)
        r_shape = lambda dt: jax.ShapeDtypeStruct((rows, D_MODEL), dt)
        vspec = pl.BlockSpec((tm, ATTN_V_WIDTH), row)
        v_shape = jax.ShapeDtypeStruct((rows, ATTN_V_WIDTH), BF16)
        out_specs = [rowspec, tspec, tspec, rowspec, vspec, rowspec, tspec, rowspec, rowspec, rowspec, rowspec]
        out_shape = [r_shape(BF16), t_shape(F32), t_shape(BF16), r_shape(F32), v_shape, r_shape(BF16),
                     t_shape(BF16), r_shape(BF16), r_shape(BF16), r_shape(BF16), r_shape(BF16)]
    else:
        out_specs = [rowspec] * N_PROJ
        out_shape = [jax.ShapeDtypeStruct((rows, D_MODEL), F32)] * N_PROJ
    return pl.pallas_call(
        functools.partial(_proj_kernel, transposed=transposed),
        grid=(rows // tm,),
        in_specs=[
            rowspec,
            _mod_spec(mod_kind, tm, rows_per_mod, 3),
            _mod_spec(mod_kind, tm, rows_per_mod, 4),
            pl.BlockSpec((1, D_MODEL), const),
            pl.BlockSpec((D_MODEL, N_PROJ * D_MODEL), const, pipeline_mode=pl.Buffered(1)),
            tab(LANES), tab(LANES), tab(RET_DK), tab(RET_DK),
        ],
        out_specs=out_specs,
        out_shape=out_shape,
        compiler_params=_cparams(("parallel",)),
        name="mixer_proj",
    )(x, mod, mod, norm_g, w_in, *tables)


def _lambda(lq1, lk1, lq2, lk2, lam_init):
    return (jnp.exp(jnp.sum(lq1 * lk1, axis=-1, keepdims=True))
            - jnp.exp(jnp.sum(lq2 * lk2, axis=-1, keepdims=True)) + lam_init)


def _attn_kernel(qi_ref, kj_ref, q_ref, kt_ref, v_ref, lq1_ref, lk1_ref, lq2_ref, lk2_ref, sg_ref, o_ref,
                 qz_ref, m_ref, acc_ref, *, tq, lam_init):
    step = pl.program_id(1)
    qi = qi_ref[step]
    kj = kj_ref[step]
    lane = lax.broadcasted_iota(jnp.int32, (tq, DA_DV), 1)
    n_rep = tq // LANES

    @pl.when(kj == 0)
    def _():
        for h in range(DA_HEADS):
            qh = q_ref[:, h * DA_DV:(h + 1) * DA_DV]
            qz_ref[h, :tq, :] = jnp.where(lane < DA_DK, qh, jnp.zeros_like(qh))
            qz_ref[h, tq:, :] = jnp.where(lane >= DA_DK, qh, jnp.zeros_like(qh))

    def block(masked, first):
        if masked:
            r = lax.broadcasted_iota(jnp.int32, (2 * tq, tq), 0) % tq
            c = lax.broadcasted_iota(jnp.int32, (2 * tq, tq), 1)
            keep = c <= r
        def scores(h):
            return jnp.dot(qz_ref[h], kt_ref[h * DA_DV:(h + 1) * DA_DV, :], preferred_element_type=F32)

        s_next = scores(0)
        for h in range(DA_HEADS):
            s = s_next
            if h + 1 < DA_HEADS:
                s_next = scores(h + 1)
            if masked:
                s = jnp.where(keep, s, NEG_INF)
            row_max = jnp.max(s, axis=-1, keepdims=True)
            if first:
                m_new = jnp.broadcast_to(row_max, (2 * tq, LANES))
            else:
                m_old = m_ref[h]
                m_new = jnp.maximum(m_old, row_max)
            p = jnp.exp2(s - jnp.concatenate([m_new] * n_rep, axis=1))
            pv = jnp.dot(p.astype(BF16), v_ref[:, h * 2 * DA_DV:(h + 1) * 2 * DA_DV], preferred_element_type=F32)
            if first:
                acc_ref[h] = pv
            else:
                alpha = jnp.exp2(m_old - m_new)
                acc_ref[h] = jnp.concatenate([alpha, alpha], axis=1) * acc_ref[h] + pv
            m_ref[h] = m_new

    for masked in (False, True):
        for first in (False, True):
            @pl.when(((kj == qi) if masked else (kj < qi)) & ((kj == 0) if first else (kj > 0)))
            def _(masked=masked, first=first):
                block(masked, first)

    @pl.when(kj == qi)
    def _():
        lam = _lambda(lq1_ref[...], lk1_ref[...], lq2_ref[...], lk2_ref[...], lam_init)
        for h in range(DA_HEADS):
            o = acc_ref[h, :, :DA_DV] / acc_ref[h, :, DA_DV:]
            d = o[:tq] - lam * o[tq:]
            o_ref[:, h * DA_DV:(h + 1) * DA_DV] = (_rms(d, sg_ref[...], SUBLN_EPS) * (1.0 - lam_init)).astype(o_ref.dtype)


def _attn_prompt(q, kt, v, lam_vecs, subln_g, batch, seq, lam_init):
    tq = min(ATTN_TQ, seq)
    nq = seq // tq
    pairs = [(i, j) for i in range(nq) for j in range(i + 1)]
    qi = jnp.asarray([p[0] for p in pairs], jnp.int32)
    kj = jnp.asarray([p[1] for p in pairs], jnp.int32)
    vec = lambda w: pl.BlockSpec((1, w), lambda b, s, qi, kj: (0, 0))
    grid_spec = pltpu.PrefetchScalarGridSpec(
        num_scalar_prefetch=2,
        grid=(batch, len(pairs)),
        in_specs=[
            pl.BlockSpec((tq, D_MODEL), lambda b, s, qi, kj: (b * nq + qi[s], 0)),
            pl.BlockSpec((None, D_MODEL, tq), lambda b, s, qi, kj: (b, 0, kj[s])),
            pl.BlockSpec((tq, ATTN_V_WIDTH), lambda b, s, qi, kj: (b * nq + kj[s], 0)),
            vec(DA_DK), vec(DA_DK), vec(DA_DK), vec(DA_DK), vec(DA_DV),
        ],
        out_specs=pl.BlockSpec((tq, D_MODEL), lambda b, s, qi, kj: (b * nq + qi[s], 0)),
        scratch_shapes=[
            pltpu.VMEM((DA_HEADS, 2 * tq, DA_DV), BF16),
            pltpu.VMEM((DA_HEADS, 2 * tq, LANES), F32),
            pltpu.VMEM((DA_HEADS, 2 * tq, 2 * DA_DV), F32),
        ],
    )
    return pl.pallas_call(
        functools.partial(_attn_kernel, tq=tq, lam_init=lam_init),
        grid_spec=grid_spec,
        out_shape=jax.ShapeDtypeStruct((batch * seq, D_MODEL), BF16),
        compiler_params=_cparams(("parallel", "arbitrary")),
        name="diff_attn_prompt",
    )(qi, kj, q, kt, v, *lam_vecs, subln_g)


def _attn_sample_kernel(pt_ref, *refs, n_pages, page, dec_seq, lam_init):
    k_refs = refs[:n_pages]
    v_refs = refs[n_pages:2 * n_pages]
    (q_ref, kn_ref, vn_ref, lq1_ref, lk1_ref, lq2_ref, lk2_ref, sg_ref, o_ref, s_ref) = refs[2 * n_pages:]
    del pt_ref
    rows = 2 * dec_seq
    q = q_ref[...].astype(BF16)
    for i in range(n_pages):
        s_ref[:, :, i * page:(i + 1) * page] = jnp.einsum(
            "hrk,hkt->hrt", q, k_refs[i][...].astype(BF16), preferred_element_type=F32)
    qf = q.astype(F32)
    kn = kn_ref[...].astype(BF16).astype(F32)
    vn = vn_ref[...].astype(BF16).astype(F32)
    r = lax.broadcasted_iota(jnp.int32, (DA_HEADS, rows, 1), 1) % dec_seq
    s_new = [jnp.where(r >= t, jnp.sum(qf * kn[:, t:t + 1, :], axis=-1, keepdims=True), NEG_INF)
             for t in range(dec_seq)]
    s_old = s_ref[...]
    m = jnp.max(s_old, axis=-1, keepdims=True)
    for s_t in s_new:
        m = jnp.maximum(m, s_t)
    p_old = jnp.exp2(s_old - m)
    denom = jnp.sum(p_old, axis=-1, keepdims=True)
    acc_new = jnp.zeros((DA_HEADS, rows, DA_DV), F32)
    for t, s_t in enumerate(s_new):
        p_t = jnp.exp2(s_t - m)
        denom = denom + p_t
        acc_new = acc_new + p_t.astype(BF16).astype(F32) * vn[:, t:t + 1, :]
    s_ref[...] = p_old
    lam = _lambda(lq1_ref[...], lk1_ref[...], lq2_ref[...], lk2_ref[...], lam_init)
    for h in range(DA_HEADS):
        acc = acc_new[h]
        for i in range(n_pages):
            acc = acc + jnp.dot(s_ref[h, :, i * page:(i + 1) * page].astype(BF16),
                                v_refs[i][pl.ds(h, page, stride=DA_HEADS), :].astype(BF16),
                                preferred_element_type=F32)
        o = acc / denom[h]
        d = o[:dec_seq] - lam * o[dec_seq:]
        o_ref[:, h * DA_DV:(h + 1) * DA_DV] = _rms(d, sg_ref[...], SUBLN_EPS) * (1.0 - lam_init)


def _attn_sample(page_table, cache_kt, cache_v, q_bd, k_new, v_new, lam_vecs, subln_g, lam_init):
    dec_batch, n_pages = page_table.shape
    page = cache_v.shape[1]
    dec_seq = k_new.shape[2]
    kspec = lambda i: pl.BlockSpec((None, DA_HEADS, DA_DV, page), lambda b, pt: (pt[b, i], 0, 0, 0))
    cache_v = cache_v.reshape(cache_v.shape[0], page * DA_HEADS, DA_DV)
    vspec = lambda i: pl.BlockSpec((None, page * DA_HEADS, DA_DV), lambda b, pt: (pt[b, i], 0, 0))
    per_b = lambda r: pl.BlockSpec((None, DA_HEADS, r, DA_DV), lambda b, pt: (b, 0, 0, 0))
    vec = lambda w: pl.BlockSpec((1, w), lambda b, pt: (0, 0))
    grid_spec = pltpu.PrefetchScalarGridSpec(
        num_scalar_prefetch=1,
        grid=(dec_batch,),
        in_specs=[kspec(i) for i in range(n_pages)] + [vspec(i) for i in range(n_pages)] + [
            per_b(2 * dec_seq), per_b(dec_seq), per_b(dec_seq),
            vec(DA_DK), vec(DA_DK), vec(DA_DK), vec(DA_DK), vec(DA_DV)],
        out_specs=pl.BlockSpec((None, dec_seq, D_MODEL), lambda b, pt: (b, 0, 0)),
        scratch_shapes=[pltpu.VMEM((DA_HEADS, 2 * dec_seq, n_pages * page), F32)],
    )
    return pl.pallas_call(
        functools.partial(_attn_sample_kernel, n_pages=n_pages, page=page, dec_seq=dec_seq, lam_init=lam_init),
        grid_spec=grid_spec,
        out_shape=jax.ShapeDtypeStruct((dec_batch, dec_seq, D_MODEL), F32),
        compiler_params=_cparams(("parallel",)),
        name="diff_attn_sample",
    )(page_table, *([cache_kt] * n_pages), *([cache_v] * n_pages), q_bd, k_new, v_new, *lam_vecs, subln_g)


def _group_norm(o, g):
    mu = jnp.mean(o, axis=-1, keepdims=True)
    d = o - mu
    return d * lax.rsqrt(jnp.mean(d * d, axis=-1, keepdims=True) + GN_EPS) * g


def _ret_prompt_kernel(lg_ref, q_ref, kt_ref, v_ref, g_ref, o_ref, st_ref, decay_ref, *, chunk):
    c = pl.program_id(0)
    batch = q_ref.shape[0]

    @pl.when(c == 0)
    def _():
        st_ref[...] = jnp.zeros(st_ref.shape, F32)
        ri = lax.broadcasted_iota(jnp.int32, (chunk, chunk), 0)
        ci = lax.broadcasted_iota(jnp.int32, (chunk, chunk), 1)
        dist = (ri - ci).astype(F32)
        for h in range(RET_HEADS):
            decay_ref[h] = jnp.where(dist >= 0, jnp.exp(jnp.maximum(dist, 0.0) * lg_ref[h]), 0.0)

    row = lax.broadcasted_iota(jnp.int32, (chunk, 1), 0)
    col = lax.broadcasted_iota(jnp.int32, (1, chunk), 1)
    for h in range(RET_HEADS):
        lg = lg_ref[h]
        sl = slice(h * RET_DK, (h + 1) * RET_DK)
        q_decay = jnp.exp((row + 1).astype(F32) * lg)
        k_decay = jnp.exp((chunk - 1 - col).astype(F32) * lg)
        carry = jnp.exp(jnp.full((1, 1), chunk, F32) * lg)
        for b in range(batch):
            q = q_ref[b, :, sl]
            kt = kt_ref[b, sl, :]
            v = v_ref[b, :, sl]
            scores = jnp.dot(q, kt, preferred_element_type=F32) * decay_ref[h]
            inner = jnp.dot(scores.astype(BF16), v, preferred_element_type=F32)
            state = st_ref[b, h]
            cross = jnp.dot(q, state.astype(BF16), preferred_element_type=F32) * q_decay
            o_ref[b, :, sl] = _group_norm(inner + cross, g_ref[:, sl]).astype(o_ref.dtype)
            ktd = (kt.astype(F32) * k_decay).astype(BF16)
            st_ref[b, h] = carry * state + jnp.dot(ktd, v, preferred_element_type=F32)


def _ret_prompt(log_g, q, kt, v, gn_g, batch, seq):
    chunk = min(RET_L, seq)
    tok = pl.BlockSpec((batch, chunk, D_MODEL), lambda c: (0, c, 0))
    o_r, state = pl.pallas_call(
        functools.partial(_ret_prompt_kernel, chunk=chunk),
        grid=(seq // chunk,),
        in_specs=[
            pl.BlockSpec(memory_space=pltpu.SMEM),
            tok,
            pl.BlockSpec((batch, D_MODEL, chunk), lambda c: (0, 0, c)),
            tok,
            pl.BlockSpec((1, D_MODEL), lambda c: (0, 0)),
        ],
        out_specs=[tok, pl.BlockSpec((batch, RET_HEADS, RET_DK, RET_DV), lambda c: (0, 0, 0, 0))],
        out_shape=[jax.ShapeDtypeStruct((batch, seq, D_MODEL), BF16),
                   jax.ShapeDtypeStruct((batch, RET_HEADS, RET_DK, RET_DV), F32)],
        scratch_shapes=[pltpu.VMEM((RET_HEADS, chunk, chunk), F32)],
        compiler_params=_cparams(("arbitrary",)),
        name="retention_prompt",
    )(log_g, q.reshape(batch, seq, D_MODEL), kt, v.reshape(batch, seq, D_MODEL), gn_g)
    return o_r.reshape(batch * seq, D_MODEL), state


def _ret_sample_kernel(lg_ref, q_ref, k_ref, v_ref, s_ref, g_ref, o_ref, st_ref, *, dec_seq):
    n_b, rows = q_ref.shape[:2]
    ri = lax.broadcasted_iota(jnp.int32, (rows, 1), 0)
    zpad = jnp.zeros((LANES - rows, RET_DK), F32)
    for h in range(RET_HEADS):
        lg = lg_ref[h]
        sl = slice(h * RET_DK, (h + 1) * RET_DK)
        decay = []
        for j in range(dec_seq):
            dist = (ri - j).astype(F32)
            decay.append(jnp.where(dist >= 0, jnp.exp(jnp.maximum(dist, 0.0) * lg), 0.0))
        q_decay = jnp.exp((ri + 1).astype(F32) * lg)
        k_decay = jnp.where(ri < dec_seq, jnp.exp((dec_seq - 1 - ri).astype(F32) * lg), 0.0)
        carry = jnp.exp(jnp.full((1, 1), dec_seq, F32) * lg)
        for b in range(n_b):
            q = q_ref[b, :, sl]
            k = k_ref[b, :, sl]
            v = v_ref[b, :, sl]
            qb = q.astype(BF16).astype(F32)
            kb = k.astype(BF16).astype(F32)
            vb = v.astype(BF16).astype(F32)
            state = s_ref[b, h]
            inner = jnp.zeros((rows, RET_DV), F32)
            for j in range(dec_seq):
                score = jnp.sum(qb * kb[j:j + 1, :], axis=-1, keepdims=True) * decay[j]
                inner = inner + score.astype(BF16).astype(F32) * vb[j:j + 1, :]
            cross = jnp.dot(q.astype(BF16), state.astype(BF16), preferred_element_type=F32) * q_decay
            o_ref[b, :, sl] = _group_norm(inner + cross, g_ref[:, sl])
            kd_t = jnp.concatenate([k * k_decay, zpad], axis=0).T.astype(BF16)
            v_pad = jnp.concatenate([v, zpad], axis=0).astype(BF16)
            st_ref[b, h] = carry * state + jnp.dot(kd_t, v_pad, preferred_element_type=F32)


def _ret_sample(log_g, q, k, v, state, gn_g, dec_seq):
    dec_batch, rows = q.shape[:2]
    n_b = math.gcd(dec_batch, RET_SAMPLE_NB)
    tok = pl.BlockSpec((n_b, rows, D_MODEL), lambda b: (b, 0, 0))
    st = pl.BlockSpec((n_b, RET_HEADS, RET_DK, RET_DV), lambda b: (b, 0, 0, 0))
    return pl.pallas_call(
        functools.partial(_ret_sample_kernel, dec_seq=dec_seq),
        grid=(dec_batch // n_b,),
        in_specs=[pl.BlockSpec(memory_space=pltpu.SMEM), tok, tok, tok, st,
                  pl.BlockSpec((1, D_MODEL), lambda b: (0, 0))],
        out_specs=[tok, st],
        out_shape=[jax.ShapeDtypeStruct((dec_batch, rows, D_MODEL), F32),
                   jax.ShapeDtypeStruct(state.shape, F32)],
        compiler_params=_cparams(("parallel",)),
        name="retention_sample",
    )(log_g, q, k, v, state, gn_g)


def kernel(x_prompt, x_sample, cache_k, cache_v, state_ret, page_table, c_prompt, c_sample, ada_w, ada_b, norm_ffn1, norm_mix, norm_ffn2, ffn1_w_in, ffn1_w_out, ffn2_w_in, ffn2_w_out, w_in, w_out, lam_q1, lam_k1, lam_q2, lam_k2, subln_g, ret_norm_g, norm_final):
    batch, seq, _ = x_prompt.shape
    dec_batch, dec_seq, _ = x_sample.shape
    depth = ada_w.shape[0]
    page = cache_k.shape[2]
    past_len = page_table.shape[1] * page
    log_g = jnp.log1p(-jnp.exp2(-5.0 - jnp.arange(RET_HEADS, dtype=F32)))
    nf = norm_final[None, :]

    tab_p = _rotation_tables(seq, 0)
    sub = 8
    tab_s = [jnp.tile(t[:dec_seq], (dec_batch, 1)) for t in _rotation_tables(sub, past_len)]

    rows_s = dec_batch * dec_seq
    n_c = rows_s + batch
    c_all = jnp.concatenate([jnp.repeat(c_sample, dec_seq, axis=0), c_prompt,
                             jnp.zeros((-n_c % sub, D_MODEL), F32)], axis=0)

    yp = x_prompt.reshape(batch * seq, D_MODEL)
    ys = x_sample.reshape(dec_batch * dec_seq, D_MODEL)
    kp_l, vp_l, sp_l, ks_l, vs_l, ss_l = [], [], [], [], [], []
    for l in range(depth):
        lam_init = 0.8 - 0.6 * math.exp(-0.3 * l)
        mod_s = _adaln(c_all, ada_w[l], ada_b[l][None, :])
        mod_p = mod_s[rows_s:n_c].reshape(batch, 1, N_MOD * D_MODEL)
        w1i, w1o = ffn1_w_in[l].astype(BF16), ffn1_w_out[l].astype(BF16)
        w2i, w2o = ffn2_w_in[l].astype(BF16), ffn2_w_out[l].astype(BF16)
        wi, wo = w_in[l].astype(BF16), w_out[l].astype(BF16)
        g1, gm, g2 = norm_ffn1[l][None, :], norm_mix[l][None, :], norm_ffn2[l][None, :]
        lam_vecs = (lam_q1[l][None, :], lam_k1[l][None, :], lam_q2[l][None, :], lam_k2[l][None, :])
        sg = subln_g[l][None, :]
        gn = ret_norm_g[l][None, :]
        last = l == depth - 1

        x1 = _ffn(yp, mod_p, "batch", seq, (0, 1, 2), g1, w1i, w1o, nf, False)
        (q, ktf, ktb, vf, vb, qr, krt, vr, gr, ga, gb) = _proj(
            x1, mod_p, "batch", seq, gm, wi, tab_p, seq // min(PROJ_TM, seq), True, batch, seq)
        o_a = _attn_prompt(q, ktb, vb, lam_vecs, sg, batch, seq, lam_init)
        o_r, sp = _ret_prompt(log_g, qr, krt, vr, gn, batch, seq)
        yp = _ffn(x1, mod_p, "batch", seq, (6, 7, 8), g2, w2i, w2o, nf, last, mixer=(5, o_a, o_r, gr, ga, gb, wo))
        kp_l.append(jnp.transpose(ktf.reshape(batch, 2 * DA_HEADS, DA_DK, seq), (0, 3, 1, 2)))
        vp_l.append(vf.reshape(batch, seq, DA_HEADS, DA_DV))
        sp_l.append(sp)

        x1 = _ffn(ys, mod_s, "token", 1, (0, 1, 2), g1, w1i, w1o, nf, False)
        (q, k, v, qr, kr, vr, gr, ga, gb) = _proj(
            x1, mod_s, "token", 1, gm, wi, tab_s, rows_s // min(PROJ_TM, rows_s), False, dec_batch, dec_seq)
        qh = q.reshape(dec_batch, dec_seq, DA_HEADS, 2, DA_DK).transpose(0, 2, 3, 1, 4)
        eye = jnp.eye(2, dtype=F32)
        q_bd = (qh[:, :, :, :, None, :] * eye[None, None, :, None, :, None]).reshape(
            dec_batch, DA_HEADS, 2 * dec_seq, DA_DV)
        to_heads = lambda a: a.reshape(dec_batch, dec_seq, DA_HEADS, DA_DV).transpose(0, 2, 1, 3)
        cache_kt = jnp.transpose(cache_k[l], (0, 2, 3, 1)).reshape(-1, DA_HEADS, DA_DV, page)
        o_a = _attn_sample(page_table, cache_kt, cache_v[l], q_bd, to_heads(k), to_heads(v), lam_vecs, sg, lam_init)
        pad = lambda a: jnp.pad(a.reshape(dec_batch, dec_seq, D_MODEL), ((0, 0), (0, sub - dec_seq), (0, 0)))
        o_r, ssm = _ret_sample(log_g, pad(qr), pad(kr), pad(vr), state_ret[l], gn, dec_seq)
        o_r = o_r[:, :dec_seq].reshape(rows_s, D_MODEL)
        ys = _ffn(x1, mod_s, "token", 1, (6, 7, 8), g2, w2i, w2o, nf, last,
                  mixer=(5, o_a.reshape(rows_s, D_MODEL), o_r, gr, ga, gb, wo))
        ks_l.append(k.reshape(dec_batch, dec_seq, 2 * DA_HEADS, DA_DK))
        vs_l.append(v.reshape(dec_batch, dec_seq, DA_HEADS, DA_DV))
        ss_l.append(ssm)

    return (yp.reshape(batch, seq, D_MODEL), ys.reshape(dec_batch, dec_seq, D_MODEL),
            jnp.stack(kp_l), jnp.stack(vp_l), jnp.stack(sp_l), jnp.stack(ks_l), jnp.stack(vs_l), jnp.stack(ss_l))
```

```python
import functools
import math

import jax
import jax.numpy as jnp
from jax import lax
from jax.experimental import pallas as pl
from jax.experimental.pallas import tpu as pltpu

F32 = jnp.float32
BF16 = jnp.bfloat16

D_MODEL = 1024
DA_DK = 64
DA_DV = 128
DA_HEADS = 8
RET_HEADS = 4
RET_DK = 256
RET_DV = 256
D_FF = 2816
N_MOD = 9
N_PROJ = 9
HALF_STEP = 0.5
ROPE_THETA = 10000.0
RET_THETA = 10000.0
NORM_EPS = 1e-6
SUBLN_EPS = 1e-5
GN_EPS = 1e-5
NEG_INF = -1e30
LOG2E = 1.4426950408889634
LANES = 128
VMEM_LIMIT = 56 * 1024 * 1024

FFN_TM = 512
FFN_TF = 256
PROJ_TM = 256
ATTN_TQ = 512
ATTN_V_WIDTH = 2 * DA_HEADS * DA_DV
RET_L = 512
RET_SAMPLE_NB = 8


def _cparams(sem):
    return pltpu.CompilerParams(dimension_semantics=sem, vmem_limit_bytes=VMEM_LIMIT)


def _rms(x, g, eps):
    return x * lax.rsqrt(jnp.mean(x * x, axis=-1, keepdims=True) + eps) * g


def _mod_spec(kind, tm, rows_per_mod, piece):
    if kind == "batch":
        return pl.BlockSpec((None, 1, D_MODEL), lambda i, *_: ((i * tm) // rows_per_mod, 0, piece))
    return pl.BlockSpec((tm, D_MODEL), lambda i, *_: (i, piece))


def _adaln_kernel(c_ref, w_ref, b_ref, o_ref):
    c = c_ref[...]
    a = (c * jax.nn.sigmoid(c)).astype(BF16)
    o_ref[...] = jnp.dot(a, w_ref[...].astype(BF16), preferred_element_type=F32) + b_ref[...]


def _adaln(c, w, b):
    m = c.shape[0]
    n = w.shape[1]
    tn = 1536
    return pl.pallas_call(
        _adaln_kernel,
        grid=(n // tn,),
        in_specs=[
            pl.BlockSpec((m, D_MODEL), lambda j: (0, 0)),
            pl.BlockSpec((D_MODEL, tn), lambda j: (0, j)),
            pl.BlockSpec((1, tn), lambda j: (0, j)),
        ],
        out_specs=pl.BlockSpec((m, tn), lambda j: (0, j)),
        out_shape=jax.ShapeDtypeStruct((m, n), F32),
        compiler_params=_cparams(("arbitrary",)),
        name="adaln",
    )(c, w, b)


def _merged_mixer(gtm_ref, oa_ref, or_ref, gr_ref, ga_ref, gb_ref, wm_ref):
    o_r = or_ref[...].astype(F32) * gr_ref[...].astype(F32)
    merged = ga_ref[...].astype(F32) * oa_ref[...].astype(F32) + gb_ref[...].astype(F32) * o_r
    return gtm_ref[...] * jnp.dot(merged.astype(BF16), wm_ref[...], preferred_element_type=F32)


def _ffn_kernel(*refs, final_norm, merge):
    if merge:
        (x_ref, gtm_ref, oa_ref, or_ref, gr_ref, ga_ref, gb_ref, wm_ref), refs = refs[:8], refs[8:]
    else:
        x_ref, refs = refs[0], refs[1:]
    sh_ref, sc_ref, gt_ref, g_ref, wi_ref, wo_ref, nf_ref, o_ref = refs
    x = x_ref[...]
    if merge:
        x = x + _merged_mixer(gtm_ref, oa_ref, or_ref, gr_ref, ga_ref, gb_ref, wm_ref)
    h = (_rms(x, g_ref[...], NORM_EPS) * (1.0 + sc_ref[...]) + sh_ref[...]).astype(BF16)
    acc = None
    for c in range(D_FF // FFN_TF):
        lo = c * FFN_TF
        a = jnp.dot(h, wi_ref[:, lo:lo + FFN_TF], preferred_element_type=F32)
        b = jnp.dot(h, wi_ref[:, D_FF + lo:D_FF + lo + FFN_TF], preferred_element_type=F32)
        act = (a * jax.nn.sigmoid(a) * b).astype(BF16)
        part = jnp.dot(act, wo_ref[lo:lo + FFN_TF, :], preferred_element_type=F32)
        acc = part if acc is None else acc + part
    out = x + HALF_STEP * gt_ref[...] * acc
    if final_norm:
        out = _rms(out, nf_ref[...], NORM_EPS)
    o_ref[...] = out


def _ffn(x, mod, mod_kind, rows_per_mod, pieces, norm_g, w_in, w_out, norm_final, final_norm, mixer=None):
    rows = x.shape[0]
    tm = min(FFN_TM, rows)
    if mixer is not None and mixer[1].dtype == F32:
        tm = tm // 2
    const = lambda i: (0, 0)
    rowspec = pl.BlockSpec((tm, D_MODEL), lambda i: (i, 0))
    resident = lambda shape: pl.BlockSpec(shape, const, pipeline_mode=pl.Buffered(1))
    in_specs = [rowspec]
    args = [x]
    if mixer is not None:
        in_specs += [_mod_spec(mod_kind, tm, rows_per_mod, mixer[0])] + [rowspec] * 5 + [
            resident((D_MODEL, D_MODEL))]
        args += [mod, *mixer[1:]]
    in_specs += [
        _mod_spec(mod_kind, tm, rows_per_mod, pieces[0]),
        _mod_spec(mod_kind, tm, rows_per_mod, pieces[1]),
        _mod_spec(mod_kind, tm, rows_per_mod, pieces[2]),
        pl.BlockSpec((1, D_MODEL), const),
        resident((D_MODEL, 2 * D_FF)),
        resident((D_FF, D_MODEL)),
        pl.BlockSpec((1, D_MODEL), const),
    ]
    args += [mod, mod, mod, norm_g, w_in, w_out, norm_final]
    return pl.pallas_call(
        functools.partial(_ffn_kernel, final_norm=final_norm, merge=mixer is not None),
        grid=(rows // tm,),
        in_specs=in_specs,
        out_specs=rowspec,
        out_shape=jax.ShapeDtypeStruct((rows, D_MODEL), F32),
        compiler_params=_cparams(("parallel",)),
        name="ffn",
    )(*args)


def _table_kernel(inv_r_ref, sgn_r_ref, inv_t_ref, sgn_t_ref, cr_ref, sr_ref, ct_ref, st_ref,
                  lcr_ref, lsr_ref, lct_ref, lst_ref, *, offset):
    rows = cr_ref.shape[0]
    i = pl.program_id(0)

    @pl.when(i == 0)
    def _():
        local = lax.broadcasted_iota(jnp.int32, (rows, 1), 0).astype(F32)
        ang = local * inv_r_ref[...]
        lcr_ref[...] = jnp.cos(ang)
        lsr_ref[...] = jnp.sin(ang)
        ang = local * inv_t_ref[...]
        lct_ref[...] = jnp.cos(ang)
        lst_ref[...] = jnp.sin(ang)

    base = jnp.full((1, 1), i * rows + offset, jnp.int32).astype(F32)
    for inv_ref, sgn_ref, lc_ref, ls_ref, c_ref, s_ref in (
            (inv_r_ref, sgn_r_ref, lcr_ref, lsr_ref, cr_ref, sr_ref),
            (inv_t_ref, sgn_t_ref, lct_ref, lst_ref, ct_ref, st_ref)):
        ang = base * inv_ref[...]
        cb, sb = jnp.cos(ang), jnp.sin(ang)
        lc, ls = lc_ref[...], ls_ref[...]
        c_ref[...] = lc * cb - ls * sb
        s_ref[...] = (ls * cb + lc * sb) * sgn_ref[...]


def _rotation_tables(n_pos, offset):
    lane = jnp.arange(LANES)
    half = DA_DK // 2
    inv_r = (ROPE_THETA ** (-((lane % DA_DK) % half).astype(F32) / half))[None, :]
    sgn_r = jnp.where((lane % DA_DK) < half, -1.0, 1.0).astype(F32)[None, :]
    lane_t = jnp.arange(RET_DK)
    angle = 1.0 / (RET_THETA ** jnp.linspace(0.0, 1.0, RET_DK // 2, dtype=F32))
    inv_t = angle[lane_t // 2][None, :]
    sgn_t = jnp.where(lane_t % 2 == 0, -1.0, 1.0).astype(F32)[None, :]
    rows = min(n_pos, 512)
    vec = lambda w: pl.BlockSpec((1, w), lambda i: (0, 0))
    tab = lambda w: pl.BlockSpec((rows, w), lambda i: (i, 0))
    return pl.pallas_call(
        functools.partial(_table_kernel, offset=offset),
        grid=(n_pos // rows,),
        in_specs=[vec(LANES), vec(LANES), vec(RET_DK), vec(RET_DK)],
        out_specs=[tab(LANES), tab(LANES), tab(RET_DK), tab(RET_DK)],
        out_shape=[jax.ShapeDtypeStruct((n_pos, LANES), F32), jax.ShapeDtypeStruct((n_pos, LANES), F32),
                   jax.ShapeDtypeStruct((n_pos, RET_DK), F32), jax.ShapeDtypeStruct((n_pos, RET_DK), F32)],
        scratch_shapes=[pltpu.VMEM((rows, LANES), F32), pltpu.VMEM((rows, LANES), F32),
                        pltpu.VMEM((rows, RET_DK), F32), pltpu.VMEM((rows, RET_DK), F32)],
        compiler_params=_cparams(("arbitrary",)),
        name="rotation_tables",
    )(inv_r, sgn_r, inv_t, sgn_t)


def _pair_rotate(x, cos, sin_signed, shift):
    lane = lax.broadcasted_iota(jnp.int32, x.shape, 1)
    partner = jnp.where((lane % (2 * shift)) < shift, pltpu.roll(x, LANES - shift, 1), pltpu.roll(x, shift, 1))
    return x * cos + partner * sin_signed


def _proj_kernel(x_ref, sh_ref, sc_ref, g_ref, w_ref, cr_ref, sr_ref, ct_ref, st_ref, *out_refs, transposed):
    h = (_rms(x_ref[...], g_ref[...], NORM_EPS) * (1.0 + sc_ref[...]) + sh_ref[...]).astype(BF16)
    cr, sr = cr_ref[...], sr_ref[...]
    n_chunks = D_MODEL // LANES
    q_scale = (DA_DK ** -0.5) * LOG2E
    k_scale = RET_DK ** -0.5

    def group(gi):
        return jnp.dot(h, w_ref[:, gi * D_MODEL:(gi + 1) * D_MODEL], preferred_element_type=F32)

    def chunk(p, c):
        return p[:, c * LANES:(c + 1) * LANES]

    def ret_tab(t_ref, c):
        half = (c % (RET_DK // LANES)) * LANES
        return t_ref[:, half:half + LANES]

    if transposed:
        (q_ref, ktf_ref, ktb_ref, vf_ref, vb_ref, qr_ref, krt_ref, vr_ref, gr_ref, ga_ref, gb_ref) = out_refs
    else:
        (q_ref, k_ref, v_ref, qr_ref, kr_ref, vr_ref, gr_ref, ga_ref, gb_ref) = out_refs

    p = group(0)
    for c in range(n_chunks):
        q_ref[:, c * LANES:(c + 1) * LANES] = (_pair_rotate(chunk(p, c), cr, sr, DA_DK // 2) * q_scale).astype(q_ref.dtype)
    p = group(1)
    for c in range(n_chunks):
        kc = _pair_rotate(chunk(p, c), cr, sr, DA_DK // 2)
        if transposed:
            kt = kc.T
            ktf_ref[c * LANES:(c + 1) * LANES, :] = kt
            ktb_ref[c * LANES:(c + 1) * LANES, :] = kt.astype(BF16)
        else:
            k_ref[:, c * LANES:(c + 1) * LANES] = kc
    p = group(2)
    if transposed:
        vf_ref[...] = p
        ones = jnp.ones((p.shape[0], DA_DV), BF16)
        for hd in range(DA_HEADS):
            vb_ref[:, 2 * hd * DA_DV:(2 * hd + 1) * DA_DV] = chunk(p, hd).astype(BF16)
            vb_ref[:, (2 * hd + 1) * DA_DV:(2 * hd + 2) * DA_DV] = ones
    else:
        v_ref[...] = p
    p = group(3)
    for c in range(n_chunks):
        qc = _pair_rotate(chunk(p, c), ret_tab(ct_ref, c), ret_tab(st_ref, c), 1)
        qr_ref[:, c * LANES:(c + 1) * LANES] = qc.astype(qr_ref.dtype)
    p = group(4)
    for c in range(n_chunks):
        kc = _pair_rotate(chunk(p, c), ret_tab(ct_ref, c), ret_tab(st_ref, c), 1) * k_scale
        if transposed:
            krt_ref[c * LANES:(c + 1) * LANES, :] = kc.T.astype(BF16)
        else:
            kr_ref[:, c * LANES:(c + 1) * LANES] = kc
    vr_ref[...] = group(5).astype(vr_ref.dtype)
    p = group(6)
    gr_ref[...] = (p * jax.nn.sigmoid(p)).astype(gr_ref.dtype)
    ga_ref[...] = jax.nn.sigmoid(group(7)).astype(ga_ref.dtype)
    gb_ref[...] = jax.nn.sigmoid(group(8)).astype(gb_ref.dtype)


def _proj(x, mod, mod_kind, rows_per_mod, norm_g, w_in, tables, n_tab_tiles, transposed, batch, seq):
    rows = x.shape[0]
    tm = min(PROJ_TM, rows)
    row = lambda i: (i, 0)
    const = lambda i: (0, 0)
    tab = lambda w: pl.BlockSpec((tm, w), lambda i: (i % n_tab_tiles, 0))
    rowspec = pl.BlockSpec((tm, D_MODEL), row)
    if transposed:
        tiles = seq // tm
        tspec = pl.BlockSpec((None, D_MODEL, tm), lambda i: (i // tiles, 0, i % tiles))
        t_shape = lambda dt: jax.ShapeDtypeStruct((batch, D_MODEL, seq), dt)
        r_shape = lambda dt: jax.ShapeDtypeStruct((rows, D_MODEL), dt)
        vspec = pl.BlockSpec((tm, ATTN_V_WIDTH), row)
        v_shape = jax.ShapeDtypeStruct((rows, ATTN_V_WIDTH), BF16)
        out_specs = [rowspec, tspec, tspec, rowspec, vspec, rowspec, tspec, rowspec, rowspec, rowspec, rowspec]
        out_shape = [r_shape(BF16), t_shape(F32), t_shape(BF16), r_shape(F32), v_shape, r_shape(BF16),
                     t_shape(BF16), r_shape(BF16), r_shape(BF16), r_shape(BF16), r_shape(BF16)]
    else:
        out_specs = [rowspec] * N_PROJ
        out_shape = [jax.ShapeDtypeStruct((rows, D_MODEL), F32)] * N_PROJ
    return pl.pallas_call(
        functools.partial(_proj_kernel, transposed=transposed),
        grid=(rows // tm,),
        in_specs=[
            rowspec,
            _mod_spec(mod_kind, tm, rows_per_mod, 3),
            _mod_spec(mod_kind, tm, rows_per_mod, 4),
            pl.BlockSpec((1, D_MODEL), const),
            pl.BlockSpec((D_MODEL, N_PROJ * D_MODEL), const, pipeline_mode=pl.Buffered(1)),
            tab(LANES), tab(LANES), tab(RET_DK), tab(RET_DK),
        ],
        out_specs=out_specs,
        out_shape=out_shape,
        compiler_params=_cparams(("parallel",)),
        name="mixer_proj",
    )(x, mod, mod, norm_g, w_in, *tables)


def _lambda(lq1, lk1, lq2, lk2, lam_init):
    return (jnp.exp(jnp.sum(lq1 * lk1, axis=-1, keepdims=True))
            - jnp.exp(jnp.sum(lq2 * lk2, axis=-1, keepdims=True)) + lam_init)


def _attn_kernel(qi_ref, kj_ref, q_ref, kt_ref, v_ref, lq1_ref, lk1_ref, lq2_ref, lk2_ref, sg_ref, o_ref,
                 qz_ref, m_ref, acc_ref, *, tq, lam_init):
    step = pl.program_id(1)
    qi = qi_ref[step]
    kj = kj_ref[step]
    lane = lax.broadcasted_iota(jnp.int32, (tq, DA_DV), 1)
    n_rep = tq // LANES

    def block(masked, first):
        if masked:
            r = lax.broadcasted_iota(jnp.int32, (2 * tq, tq), 0) % tq
            c = lax.broadcasted_iota(jnp.int32, (2 * tq, tq), 1)
            keep = c <= r
            lam = _lambda(lq1_ref[...], lk1_ref[...], lq2_ref[...], lk2_ref[...], lam_init)

        def scores(h):
            if first:
                qh = q_ref[:, h * DA_DV:(h + 1) * DA_DV]
                qz_ref[h, :tq, :] = jnp.where(lane < DA_DK, qh, jnp.zeros_like(qh))
                qz_ref[h, tq:, :] = jnp.where(lane >= DA_DK, qh, jnp.zeros_like(qh))
            return jnp.dot(qz_ref[h], kt_ref[h * DA_DV:(h + 1) * DA_DV, :], preferred_element_type=F32)

        s_next = scores(0)
        for h in range(DA_HEADS):
            s = s_next
            if h + 1 < DA_HEADS:
                s_next = scores(h + 1)
            if masked:
                s = jnp.where(keep, s, NEG_INF)
            row_max = jnp.max(s, axis=-1, keepdims=True)
            if first:
                m_new = jnp.broadcast_to(row_max, (2 * tq, LANES))
            else:
                m_old = m_ref[h]
                m_new = jnp.maximum(m_old, row_max)
            p = jnp.exp2(s - jnp.concatenate([m_new] * n_rep, axis=1))
            pv = jnp.dot(p.astype(BF16), v_ref[:, h * 2 * DA_DV:(h + 1) * 2 * DA_DV], preferred_element_type=F32)
            if first:
                acc = pv
            else:
                alpha = jnp.exp2(m_old - m_new)
                acc = jnp.concatenate([alpha, alpha], axis=1) * acc_ref[h] + pv
            if masked:
                o = acc[:, :DA_DV] / acc[:, DA_DV:]
                d = o[:tq] - lam * o[tq:]
                o_ref[:, h * DA_DV:(h + 1) * DA_DV] = (
                    _rms(d, sg_ref[...], SUBLN_EPS) * (1.0 - lam_init)).astype(o_ref.dtype)
            else:
                acc_ref[h] = acc
                m_ref[h] = m_new

    for masked in (False, True):
        for first in (False, True):
            @pl.when(((kj == qi) if masked else (kj < qi)) & ((kj == 0) if first else (kj > 0)))
            def _(masked=masked, first=first):
                block(masked, first)


def _attn_prompt(q, kt, v, lam_vecs, subln_g, batch, seq, lam_init):
    tq = min(ATTN_TQ, seq)
    nq = seq // tq
    pairs = [(i, j) for i in range(nq) for j in range(i + 1)]
    qi = jnp.asarray([p[0] for p in pairs], jnp.int32)
    kj = jnp.asarray([p[1] for p in pairs], jnp.int32)
    vec = lambda w: pl.BlockSpec((1, w), lambda b, s, qi, kj: (0, 0))
    grid_spec = pltpu.PrefetchScalarGridSpec(
        num_scalar_prefetch=2,
        grid=(batch, len(pairs)),
        in_specs=[
            pl.BlockSpec((tq, D_MODEL), lambda b, s, qi, kj: (b * nq + qi[s], 0)),
            pl.BlockSpec((None, D_MODEL, tq), lambda b, s, qi, kj: (b, 0, kj[s])),
            pl.BlockSpec((tq, ATTN_V_WIDTH), lambda b, s, qi, kj: (b * nq + kj[s], 0)),
            vec(DA_DK), vec(DA_DK), vec(DA_DK), vec(DA_DK), vec(DA_DV),
        ],
        out_specs=pl.BlockSpec((tq, D_MODEL), lambda b, s, qi, kj: (b * nq + qi[s], 0)),
        scratch_shapes=[
            pltpu.VMEM((DA_HEADS, 2 * tq, DA_DV), BF16),
            pltpu.VMEM((DA_HEADS, 2 * tq, LANES), F32),
            pltpu.VMEM((DA_HEADS, 2 * tq, 2 * DA_DV), F32),
        ],
    )
    return pl.pallas_call(
        functools.partial(_attn_kernel, tq=tq, lam_init=lam_init),
        grid_spec=grid_spec,
        out_shape=jax.ShapeDtypeStruct((batch * seq, D_MODEL), BF16),
        compiler_params=_cparams(("parallel", "arbitrary")),
        name="diff_attn_prompt",
    )(qi, kj, q, kt, v, *lam_vecs, subln_g)


def _attn_sample_kernel(pt_ref, *refs, n_pages, page, dec_seq, lam_init):
    k_refs = refs[:n_pages]
    v_refs = refs[n_pages:2 * n_pages]
    (q_ref, kn_ref, vn_ref, lq1_ref, lk1_ref, lq2_ref, lk2_ref, sg_ref, o_ref, s_ref) = refs[2 * n_pages:]
    del pt_ref
    rows = 2 * dec_seq
    q = q_ref[...].astype(BF16)
    for i in range(n_pages):
        s_ref[:, :, i * page:(i + 1) * page] = jnp.einsum(
            "hrk,hkt->hrt", q, k_refs[i][...].astype(BF16), preferred_element_type=F32)
    qf = q.astype(F32)
    kn = kn_ref[...].astype(BF16).astype(F32)
    vn = vn_ref[...].astype(BF16).astype(F32)
    r = lax.broadcasted_iota(jnp.int32, (DA_HEADS, rows, 1), 1) % dec_seq
    s_new = [jnp.where(r >= t, jnp.sum(qf * kn[:, t:t + 1, :], axis=-1, keepdims=True), NEG_INF)
             for t in range(dec_seq)]
    s_old = s_ref[...]
    m = jnp.max(s_old, axis=-1, keepdims=True)
    for s_t in s_new:
        m = jnp.maximum(m, s_t)
    p_old = jnp.exp2(s_old - m)
    denom = jnp.sum(p_old, axis=-1, keepdims=True)
    acc_new = jnp.zeros((DA_HEADS, rows, DA_DV), F32)
    for t, s_t in enumerate(s_new):
        p_t = jnp.exp2(s_t - m)
        denom = denom + p_t
        acc_new = acc_new + p_t.astype(BF16).astype(F32) * vn[:, t:t + 1, :]
    s_ref[...] = p_old
    lam = _lambda(lq1_ref[...], lk1_ref[...], lq2_ref[...], lk2_ref[...], lam_init)
    for h in range(DA_HEADS):
        acc = acc_new[h]
        for i in range(n_pages):
            acc = acc + jnp.dot(s_ref[h, :, i * page:(i + 1) * page].astype(BF16),
                                v_refs[i][pl.ds(h, page, stride=DA_HEADS), :].astype(BF16),
                                preferred_element_type=F32)
        o = acc / denom[h]
        d = o[:dec_seq] - lam * o[dec_seq:]
        o_ref[:, h * DA_DV:(h + 1) * DA_DV] = _rms(d, sg_ref[...], SUBLN_EPS) * (1.0 - lam_init)


def _attn_sample(page_table, cache_kt, cache_v, q_bd, k_new, v_new, lam_vecs, subln_g, lam_init):
    dec_batch, n_pages = page_table.shape
    page = cache_v.shape[1]
    dec_seq = k_new.shape[2]
    kspec = lambda i: pl.BlockSpec((None, DA_HEADS, DA_DV, page), lambda b, pt: (pt[b, i], 0, 0, 0))
    cache_v = cache_v.reshape(cache_v.shape[0], page * DA_HEADS, DA_DV)
    vspec = lambda i: pl.BlockSpec((None, page * DA_HEADS, DA_DV), lambda b, pt: (pt[b, i], 0, 0))
    per_b = lambda r: pl.BlockSpec((None, DA_HEADS, r, DA_DV), lambda b, pt: (b, 0, 0, 0))
    vec = lambda w: pl.BlockSpec((1, w), lambda b, pt: (0, 0))
    grid_spec = pltpu.PrefetchScalarGridSpec(
        num_scalar_prefetch=1,
        grid=(dec_batch,),
        in_specs=[kspec(i) for i in range(n_pages)] + [vspec(i) for i in range(n_pages)] + [
            per_b(2 * dec_seq), per_b(dec_seq), per_b(dec_seq),
            vec(DA_DK), vec(DA_DK), vec(DA_DK), vec(DA_DK), vec(DA_DV)],
        out_specs=pl.BlockSpec((None, dec_seq, D_MODEL), lambda b, pt: (b, 0, 0)),
        scratch_shapes=[pltpu.VMEM((DA_HEADS, 2 * dec_seq, n_pages * page), F32)],
    )
    return pl.pallas_call(
        functools.partial(_attn_sample_kernel, n_pages=n_pages, page=page, dec_seq=dec_seq, lam_init=lam_init),
        grid_spec=grid_spec,
        out_shape=jax.ShapeDtypeStruct((dec_batch, dec_seq, D_MODEL), F32),
        compiler_params=_cparams(("parallel",)),
        name="diff_attn_sample",
    )(page_table, *([cache_kt] * n_pages), *([cache_v] * n_pages), q_bd, k_new, v_new, *lam_vecs, subln_g)


def _group_norm(o, g):
    mu = jnp.mean(o, axis=-1, keepdims=True)
    d = o - mu
    return d * lax.rsqrt(jnp.mean(d * d, axis=-1, keepdims=True) + GN_EPS) * g


def _ret_prompt_kernel(lg_ref, q_ref, kt_ref, v_ref, g_ref, o_ref, st_ref, decay_ref, *, chunk):
    c = pl.program_id(0)
    batch = q_ref.shape[0]

    @pl.when(c == 0)
    def _():
        st_ref[...] = jnp.zeros(st_ref.shape, F32)
        ri = lax.broadcasted_iota(jnp.int32, (chunk, chunk), 0)
        ci = lax.broadcasted_iota(jnp.int32, (chunk, chunk), 1)
        dist = (ri - ci).astype(F32)
        for h in range(RET_HEADS):
            decay_ref[h] = jnp.where(dist >= 0, jnp.exp(jnp.maximum(dist, 0.0) * lg_ref[h]), 0.0)

    row = lax.broadcasted_iota(jnp.int32, (chunk, 1), 0)
    col = lax.broadcasted_iota(jnp.int32, (1, chunk), 1)
    for h in range(RET_HEADS):
        lg = lg_ref[h]
        sl = slice(h * RET_DK, (h + 1) * RET_DK)
        q_decay = jnp.exp((row + 1).astype(F32) * lg)
        k_decay = jnp.exp((chunk - 1 - col).astype(F32) * lg)
        carry = jnp.exp(jnp.full((1, 1), chunk, F32) * lg)
        for b in range(batch):
            q = q_ref[b, :, sl]
            kt = kt_ref[b, sl, :]
            v = v_ref[b, :, sl]
            scores = jnp.dot(q, kt, preferred_element_type=F32) * decay_ref[h]
            inner = jnp.dot(scores.astype(BF16), v, preferred_element_type=F32)
            state = st_ref[b, h]
            cross = jnp.dot(q, state.astype(BF16), preferred_element_type=F32) * q_decay
            o_ref[b, :, sl] = _group_norm(inner + cross, g_ref[:, sl]).astype(o_ref.dtype)
            ktd = (kt.astype(F32) * k_decay).astype(BF16)
            st_ref[b, h] = carry * state + jnp.dot(ktd, v, preferred_element_type=F32)


def _ret_prompt(log_g, q, kt, v, gn_g, batch, seq):
    chunk = min(RET_L, seq)
    tok = pl.BlockSpec((batch, chunk, D_MODEL), lambda c: (0, c, 0))
    o_r, state = pl.pallas_call(
        functools.partial(_ret_prompt_kernel, chunk=chunk),
        grid=(seq // chunk,),
        in_specs=[
            pl.BlockSpec(memory_space=pltpu.SMEM),
            tok,
            pl.BlockSpec((batch, D_MODEL, chunk), lambda c: (0, 0, c)),
            tok,
            pl.BlockSpec((1, D_MODEL), lambda c: (0, 0)),
        ],
        out_specs=[tok, pl.BlockSpec((batch, RET_HEADS, RET_DK, RET_DV), lambda c: (0, 0, 0, 0))],
        out_shape=[jax.ShapeDtypeStruct((batch, seq, D_MODEL), BF16),
                   jax.ShapeDtypeStruct((batch, RET_HEADS, RET_DK, RET_DV), F32)],
        scratch_shapes=[pltpu.VMEM((RET_HEADS, chunk, chunk), F32)],
        compiler_params=_cparams(("arbitrary",)),
        name="retention_prompt",
    )(log_g, q.reshape(batch, seq, D_MODEL), kt, v.reshape(batch, seq, D_MODEL), gn_g)
    return o_r.reshape(batch * seq, D_MODEL), state


def _ret_sample_kernel(lg_ref, q_ref, k_ref, v_ref, s_ref, g_ref, o_ref, st_ref, *, dec_seq):
    n_b, rows = q_ref.shape[:2]
    ri = lax.broadcasted_iota(jnp.int32, (rows, 1), 0)
    zpad = jnp.zeros((LANES - rows, RET_DK), F32)
    for h in range(RET_HEADS):
        lg = lg_ref[h]
        sl = slice(h * RET_DK, (h + 1) * RET_DK)
        decay = []
        for j in range(dec_seq):
            dist = (ri - j).astype(F32)
            decay.append(jnp.where(dist >= 0, jnp.exp(jnp.maximum(dist, 0.0) * lg), 0.0))
        q_decay = jnp.exp((ri + 1).astype(F32) * lg)
        k_decay = jnp.where(ri < dec_seq, jnp.exp((dec_seq - 1 - ri).astype(F32) * lg), 0.0)
        carry = jnp.exp(jnp.full((1, 1), dec_seq, F32) * lg)
        for b in range(n_b):
            q = q_ref[b, :, sl]
            k = k_ref[b, :, sl]
            v = v_ref[b, :, sl]
            qb = q.astype(BF16).astype(F32)
            kb = k.astype(BF16).astype(F32)
            vb = v.astype(BF16).astype(F32)
            state = s_ref[b, h]
            inner = jnp.zeros((rows, RET_DV), F32)
            for j in range(dec_seq):
                score = jnp.sum(qb * kb[j:j + 1, :], axis=-1, keepdims=True) * decay[j]
                inner = inner + score.astype(BF16).astype(F32) * vb[j:j + 1, :]
            cross = jnp.dot(q.astype(BF16), state.astype(BF16), preferred_element_type=F32) * q_decay
            o_ref[b, :, sl] = _group_norm(inner + cross, g_ref[:, sl])
            kd_t = jnp.concatenate([k * k_decay, zpad], axis=0).T.astype(BF16)
            v_pad = jnp.concatenate([v, zpad], axis=0).astype(BF16)
            st_ref[b, h] = carry * state + jnp.dot(kd_t, v_pad, preferred_element_type=F32)


def _ret_sample(log_g, q, k, v, state, gn_g, dec_seq):
    dec_batch, rows = q.shape[:2]
    n_b = math.gcd(dec_batch, RET_SAMPLE_NB)
    tok = pl.BlockSpec((n_b, rows, D_MODEL), lambda b: (b, 0, 0))
    st = pl.BlockSpec((n_b, RET_HEADS, RET_DK, RET_DV), lambda b: (b, 0, 0, 0))
    return pl.pallas_call(
        functools.partial(_ret_sample_kernel, dec_seq=dec_seq),
        grid=(dec_batch // n_b,),
        in_specs=[pl.BlockSpec(memory_space=pltpu.SMEM), tok, tok, tok, st,
                  pl.BlockSpec((1, D_MODEL), lambda b: (0, 0))],
        out_specs=[tok, st],
        out_shape=[jax.ShapeDtypeStruct((dec_batch, rows, D_MODEL), F32),
                   jax.ShapeDtypeStruct(state.shape, F32)],
        compiler_params=_cparams(("parallel",)),
        name="retention_sample",
    )(log_g, q, k, v, state, gn_g)


def kernel(x_prompt, x_sample, cache_k, cache_v, state_ret, page_table, c_prompt, c_sample, ada_w, ada_b, norm_ffn1, norm_mix, norm_ffn2, ffn1_w_in, ffn1_w_out, ffn2_w_in, ffn2_w_out, w_in, w_out, lam_q1, lam_k1, lam_q2, lam_k2, subln_g, ret_norm_g, norm_final):
    batch, seq, _ = x_prompt.shape
    dec_batch, dec_seq, _ = x_sample.shape
    depth = ada_w.shape[0]
    page = cache_k.shape[2]
    past_len = page_table.shape[1] * page
    log_g = jnp.log1p(-jnp.exp2(-5.0 - jnp.arange(RET_HEADS, dtype=F32)))
    nf = norm_final[None, :]

    tab_p = _rotation_tables(seq, 0)
    sub = 8
    tab_s = [jnp.tile(t[:dec_seq], (dec_batch, 1)) for t in _rotation_tables(sub, past_len)]

    rows_s = dec_batch * dec_seq
    n_c = rows_s + batch
    c_all = jnp.concatenate([jnp.repeat(c_sample, dec_seq, axis=0), c_prompt,
                             jnp.zeros((-n_c % sub, D_MODEL), F32)], axis=0)

    yp = x_prompt.reshape(batch * seq, D_MODEL)
    ys = x_sample.reshape(dec_batch * dec_seq, D_MODEL)
    kp_l, vp_l, sp_l, ks_l, vs_l, ss_l = [], [], [], [], [], []
    for l in range(depth):
        lam_init = 0.8 - 0.6 * math.exp(-0.3 * l)
        mod_s = _adaln(c_all, ada_w[l], ada_b[l][None, :])
        mod_p = mod_s[rows_s:n_c].reshape(batch, 1, N_MOD * D_MODEL)
        w1i, w1o = ffn1_w_in[l].astype(BF16), ffn1_w_out[l].astype(BF16)
        w2i, w2o = ffn2_w_in[l].astype(BF16), ffn2_w_out[l].astype(BF16)
        wi, wo = w_in[l].astype(BF16), w_out[l].astype(BF16)
        g1, gm, g2 = norm_ffn1[l][None, :], norm_mix[l][None, :], norm_ffn2[l][None, :]
        lam_vecs = (lam_q1[l][None, :], lam_k1[l][None, :], lam_q2[l][None, :], lam_k2[l][None, :])
        sg = subln_g[l][None, :]
        gn = ret_norm_g[l][None, :]
        last = l == depth - 1

        x1 = _ffn(yp, mod_p, "batch", seq, (0, 1, 2), g1, w1i, w1o, nf, False)
        (q, ktf, ktb, vf, vb, qr, krt, vr, gr, ga, gb) = _proj(
            x1, mod_p, "batch", seq, gm, wi, tab_p, seq // min(PROJ_TM, seq), True, batch, seq)
        o_a = _attn_prompt(q, ktb, vb, lam_vecs, sg, batch, seq, lam_init)
        o_r, sp = _ret_prompt(log_g, qr, krt, vr, gn, batch, seq)
        yp = _ffn(x1, mod_p, "batch", seq, (6, 7, 8), g2, w2i, w2o, nf, last, mixer=(5, o_a, o_r, gr, ga, gb, wo))
        kp_l.append(jnp.transpose(ktf.reshape(batch, 2 * DA_HEADS, DA_DK, seq), (0, 3, 1, 2)))
        vp_l.append(vf.reshape(batch, seq, DA_HEADS, DA_DV))
        sp_l.append(sp)

        x1 = _ffn(ys, mod_s, "token", 1, (0, 1, 2), g1, w1i, w1o, nf, False)
        (q, k, v, qr, kr, vr, gr, ga, gb) = _proj(
            x1, mod_s, "token", 1, gm, wi, tab_s, rows_s // min(PROJ_TM, rows_s), False, dec_batch, dec_seq)
        qh = q.reshape(dec_batch, dec_seq, DA_HEADS, 2, DA_DK).transpose(0, 2, 3, 1, 4)
        eye = jnp.eye(2, dtype=F32)
        q_bd = (qh[:, :, :, :, None, :] * eye[None, None, :, None, :, None]).reshape(
            dec_batch, DA_HEADS, 2 * dec_seq, DA_DV)
        to_heads = lambda a: a.reshape(dec_batch, dec_seq, DA_HEADS, DA_DV).transpose(0, 2, 1, 3)
        cache_kt = jnp.transpose(cache_k[l], (0, 2, 3, 1)).reshape(-1, DA_HEADS, DA_DV, page)
        o_a = _attn_sample(page_table, cache_kt, cache_v[l], q_bd, to_heads(k), to_heads(v), lam_vecs, sg, lam_init)
        pad = lambda a: jnp.pad(a.reshape(dec_batch, dec_seq, D_MODEL), ((0, 0), (0, sub - dec_seq), (0, 0)))
        o_r, ssm = _ret_sample(log_g, pad(qr), pad(kr), pad(vr), state_ret[l], gn, dec_seq)
        o_r = o_r[:, :dec_seq].reshape(rows_s, D_MODEL)
        ys = _ffn(x1, mod_s, "token", 1, (6, 7, 8), g2, w2i, w2o, nf, last,
                  mixer=(5, o_a.reshape(rows_s, D_MODEL), o_r, gr, ga, gb, wo))
        ks_l.append(k.reshape(dec_batch, dec_seq, 2 * DA_HEADS, DA_DK))
        vs_l.append(v.reshape(dec_batch, dec_seq, DA_HEADS, DA_DV))
        ss_l.append(ssm)

    return (yp.reshape(batch, seq, D_MODEL), ys.reshape(dec_batch, dec_seq, D_MODEL),
            jnp.stack(kp_l), jnp.stack(vp_l), jnp.stack(sp_l), jnp.stack(ks_l), jnp.stack(vs_l), jnp.stack(ss_l))
```

```python
import functools
import math

import jax
import jax.numpy as jnp
from jax import lax
from jax.experimental import pallas as pl
from jax.experimental.pallas import tpu as pltpu

F32 = jnp.float32
BF16 = jnp.bfloat16

D_MODEL = 1024
DA_DK = 64
DA_DV = 128
DA_HEADS = 8
RET_HEADS = 4
RET_DK = 256
RET_DV = 256
D_FF = 2816
N_MOD = 9
N_PROJ = 9
HALF_STEP = 0.5
ROPE_THETA = 10000.0
RET_THETA = 10000.0
NORM_EPS = 1e-6
SUBLN_EPS = 1e-5
GN_EPS = 1e-5
NEG_INF = -1e30
LOG2E = 1.4426950408889634
LANES = 128
VMEM_LIMIT = 56 * 1024 * 1024

FFN_TM = 512
FFN_TF = 256
PROJ_TM = 256
ATTN_TQ = 512
ATTN_V_WIDTH = 2 * DA_HEADS * DA_DV
RET_L = 512
RET_SAMPLE_NB = 8


def _cparams(sem):
    return pltpu.CompilerParams(dimension_semantics=sem, vmem_limit_bytes=VMEM_LIMIT)


def _rms(x, g, eps):
    return x * lax.rsqrt(jnp.mean(x * x, axis=-1, keepdims=True) + eps) * g


def _mod_spec(kind, tm, rows_per_mod, piece):
    if kind == "batch":
        return pl.BlockSpec((None, 1, D_MODEL), lambda i, *_: ((i * tm) // rows_per_mod, 0, piece))
    return pl.BlockSpec((tm, D_MODEL), lambda i, *_: (i, piece))


def _adaln_kernel(c_ref, w_ref, b_ref, o_ref, a_ref):
    @pl.when(pl.program_id(0) == 0)
    def _():
        c = c_ref[...]
        a_ref[...] = (c * jax.nn.sigmoid(c)).astype(BF16)

    o_ref[...] = jnp.dot(a_ref[...], w_ref[...].astype(BF16), preferred_element_type=F32) + b_ref[...]


def _adaln(c, w, b):
    m = c.shape[0]
    n = w.shape[1]
    tn = 1536
    return pl.pallas_call(
        _adaln_kernel,
        grid=(n // tn,),
        in_specs=[
            pl.BlockSpec((m, D_MODEL), lambda j: (0, 0)),
            pl.BlockSpec((D_MODEL, tn), lambda j: (0, j)),
            pl.BlockSpec((1, tn), lambda j: (0, j)),
        ],
        out_specs=pl.BlockSpec((m, tn), lambda j: (0, j)),
        out_shape=jax.ShapeDtypeStruct((m, n), F32),
        scratch_shapes=[pltpu.VMEM((m, D_MODEL), BF16)],
        compiler_params=_cparams(("arbitrary",)),
        name="adaln",
    )(c, w, b)


def _merged_mixer(gtm_ref, oa_ref, or_ref, gr_ref, ga_ref, gb_ref, wm_ref):
    o_r = or_ref[...].astype(F32) * gr_ref[...].astype(F32)
    merged = ga_ref[...].astype(F32) * oa_ref[...].astype(F32) + gb_ref[...].astype(F32) * o_r
    return gtm_ref[...] * jnp.dot(merged.astype(BF16), wm_ref[...], preferred_element_type=F32)


def _ffn_kernel(*refs, final_norm, merge):
    if merge:
        (x_ref, gtm_ref, oa_ref, or_ref, gr_ref, ga_ref, gb_ref, wm_ref), refs = refs[:8], refs[8:]
    else:
        x_ref, refs = refs[0], refs[1:]
    sh_ref, sc_ref, gt_ref, g_ref, wi_ref, wo_ref, nf_ref, o_ref = refs
    x = x_ref[...]
    if merge:
        x = x + _merged_mixer(gtm_ref, oa_ref, or_ref, gr_ref, ga_ref, gb_ref, wm_ref)
    h = (_rms(x, g_ref[...], NORM_EPS) * (1.0 + sc_ref[...]) + sh_ref[...]).astype(BF16)
    acc = None
    for c in range(D_FF // FFN_TF):
        lo = c * FFN_TF
        a = jnp.dot(h, wi_ref[:, lo:lo + FFN_TF], preferred_element_type=F32)
        b = jnp.dot(h, wi_ref[:, D_FF + lo:D_FF + lo + FFN_TF], preferred_element_type=F32)
        act = (a * jax.nn.sigmoid(a) * b).astype(BF16)
        part = jnp.dot(act, wo_ref[lo:lo + FFN_TF, :], preferred_element_type=F32)
        acc = part if acc is None else acc + part
    out = x + HALF_STEP * gt_ref[...] * acc
    if final_norm:
        out = _rms(out, nf_ref[...], NORM_EPS)
    o_ref[...] = out


def _ffn(x, mod, mod_kind, rows_per_mod, pieces, norm_g, w_in, w_out, norm_final, final_norm, mixer=None):
    rows = x.shape[0]
    tm = min(FFN_TM, rows)
    const = lambda i: (0, 0)
    rowspec = pl.BlockSpec((tm, D_MODEL), lambda i: (i, 0))
    resident = lambda shape: pl.BlockSpec(shape, const, pipeline_mode=pl.Buffered(1))
    in_specs = [rowspec]
    args = [x]
    if mixer is not None:
        in_specs += [_mod_spec(mod_kind, tm, rows_per_mod, mixer[0])] + [rowspec] * 5 + [
            resident((D_MODEL, D_MODEL))]
        args += [mod, *mixer[1:]]
    in_specs += [
        _mod_spec(mod_kind, tm, rows_per_mod, pieces[0]),
        _mod_spec(mod_kind, tm, rows_per_mod, pieces[1]),
        _mod_spec(mod_kind, tm, rows_per_mod, pieces[2]),
        pl.BlockSpec((1, D_MODEL), const),
        resident((D_MODEL, 2 * D_FF)),
        resident((D_FF, D_MODEL)),
        pl.BlockSpec((1, D_MODEL), const),
    ]
    args += [mod, mod, mod, norm_g, w_in, w_out, norm_final]
    return pl.pallas_call(
        functools.partial(_ffn_kernel, final_norm=final_norm, merge=mixer is not None),
        grid=(rows // tm,),
        in_specs=in_specs,
        out_specs=rowspec,
        out_shape=jax.ShapeDtypeStruct((rows, D_MODEL), F32),
        compiler_params=_cparams(("parallel",)),
        name="ffn",
    )(*args)


def _table_kernel(inv_r_ref, sgn_r_ref, inv_t_ref, sgn_t_ref, cr_ref, sr_ref, ct_ref, st_ref,
                  lcr_ref, lsr_ref, lct_ref, lst_ref, *, offset):
    rows = cr_ref.shape[0]
    i = pl.program_id(0)

    @pl.when(i == 0)
    def _():
        local = lax.broadcasted_iota(jnp.int32, (rows, 1), 0).astype(F32)
        ang = local * inv_r_ref[...]
        lcr_ref[...] = jnp.cos(ang)
        lsr_ref[...] = jnp.sin(ang)
        ang = local * inv_t_ref[...]
        lct_ref[...] = jnp.cos(ang)
        lst_ref[...] = jnp.sin(ang)

    base = jnp.full((1, 1), i * rows + offset, jnp.int32).astype(F32)
    for inv_ref, sgn_ref, lc_ref, ls_ref, c_ref, s_ref in (
            (inv_r_ref, sgn_r_ref, lcr_ref, lsr_ref, cr_ref, sr_ref),
            (inv_t_ref, sgn_t_ref, lct_ref, lst_ref, ct_ref, st_ref)):
        ang = base * inv_ref[...]
        cb, sb = jnp.cos(ang), jnp.sin(ang)
        lc, ls = lc_ref[...], ls_ref[...]
        c_ref[...] = lc * cb - ls * sb
        s_ref[...] = (ls * cb + lc * sb) * sgn_ref[...]


def _rotation_tables(n_pos, offset):
    lane = jnp.arange(LANES)
    half = DA_DK // 2
    inv_r = (ROPE_THETA ** (-((lane % DA_DK) % half).astype(F32) / half))[None, :]
    sgn_r = jnp.where((lane % DA_DK) < half, -1.0, 1.0).astype(F32)[None, :]
    lane_t = jnp.arange(RET_DK)
    angle = 1.0 / (RET_THETA ** jnp.linspace(0.0, 1.0, RET_DK // 2, dtype=F32))
    inv_t = angle[lane_t // 2][None, :]
    sgn_t = jnp.where(lane_t % 2 == 0, -1.0, 1.0).astype(F32)[None, :]
    rows = min(n_pos, 1024)
    vec = lambda w: pl.BlockSpec((1, w), lambda i: (0, 0))
    tab = lambda w: pl.BlockSpec((rows, w), lambda i: (i, 0))
    return pl.pallas_call(
        functools.partial(_table_kernel, offset=offset),
        grid=(n_pos // rows,),
        in_specs=[vec(LANES), vec(LANES), vec(RET_DK), vec(RET_DK)],
        out_specs=[tab(LANES), tab(LANES), tab(RET_DK), tab(RET_DK)],
        out_shape=[jax.ShapeDtypeStruct((n_pos, LANES), F32), jax.ShapeDtypeStruct((n_pos, LANES), F32),
                   jax.ShapeDtypeStruct((n_pos, RET_DK), F32), jax.ShapeDtypeStruct((n_pos, RET_DK), F32)],
        scratch_shapes=[pltpu.VMEM((rows, LANES), F32), pltpu.VMEM((rows, LANES), F32),
                        pltpu.VMEM((rows, RET_DK), F32), pltpu.VMEM((rows, RET_DK), F32)],
        compiler_params=_cparams(("arbitrary",)),
        name="rotation_tables",
    )(inv_r, sgn_r, inv_t, sgn_t)


def _pair_rotate(x, cos, sin_signed, shift):
    lane = lax.broadcasted_iota(jnp.int32, x.shape, 1)
    partner = jnp.where((lane % (2 * shift)) < shift, pltpu.roll(x, LANES - shift, 1), pltpu.roll(x, shift, 1))
    return x * cos + partner * sin_signed


def _proj_kernel(x_ref, sh_ref, sc_ref, g_ref, w_ref, cr_ref, sr_ref, ct_ref, st_ref, *out_refs, transposed):
    h = (_rms(x_ref[...], g_ref[...], NORM_EPS) * (1.0 + sc_ref[...]) + sh_ref[...]).astype(BF16)
    cr, sr = cr_ref[...], sr_ref[...]
    n_chunks = D_MODEL // LANES
    q_scale = (DA_DK ** -0.5) * LOG2E
    k_scale = RET_DK ** -0.5

    def group(gi):
        return jnp.dot(h, w_ref[:, gi * D_MODEL:(gi + 1) * D_MODEL], preferred_element_type=F32)

    def chunk(p, c):
        return p[:, c * LANES:(c + 1) * LANES]

    def ret_tab(t_ref, c):
        half = (c % (RET_DK // LANES)) * LANES
        return t_ref[:, half:half + LANES]

    if transposed:
        (q_ref, ktf_ref, ktb_ref, vf_ref, vb_ref, qr_ref, krt_ref, vr_ref, gr_ref, ga_ref, gb_ref) = out_refs
    else:
        (q_ref, k_ref, v_ref, qr_ref, kr_ref, vr_ref, gr_ref, ga_ref, gb_ref) = out_refs

    p = group(0)
    for c in range(n_chunks):
        q_ref[:, c * LANES:(c + 1) * LANES] = (_pair_rotate(chunk(p, c), cr, sr, DA_DK // 2) * q_scale).astype(q_ref.dtype)
    p = group(1)
    for c in range(n_chunks):
        kc = _pair_rotate(chunk(p, c), cr, sr, DA_DK // 2)
        if transposed:
            kt = kc.T
            ktf_ref[c * LANES:(c + 1) * LANES, :] = kt
            ktb_ref[c * LANES:(c + 1) * LANES, :] = kt.astype(BF16)
        else:
            k_ref[:, c * LANES:(c + 1) * LANES] = kc
    p = group(2)
    if transposed:
        vf_ref[...] = p
        ones = jnp.ones((p.shape[0], DA_DV), BF16)
        for hd in range(DA_HEADS):
            vb_ref[:, 2 * hd * DA_DV:(2 * hd + 1) * DA_DV] = chunk(p, hd).astype(BF16)
            vb_ref[:, (2 * hd + 1) * DA_DV:(2 * hd + 2) * DA_DV] = ones
    else:
        v_ref[...] = p
    p = group(3)
    for c in range(n_chunks):
        qc = _pair_rotate(chunk(p, c), ret_tab(ct_ref, c), ret_tab(st_ref, c), 1)
        qr_ref[:, c * LANES:(c + 1) * LANES] = qc.astype(qr_ref.dtype)
    p = group(4)
    for c in range(n_chunks):
        kc = _pair_rotate(chunk(p, c), ret_tab(ct_ref, c), ret_tab(st_ref, c), 1) * k_scale
        if transposed:
            krt_ref[c * LANES:(c + 1) * LANES, :] = kc.T.astype(BF16)
        else:
            kr_ref[:, c * LANES:(c + 1) * LANES] = kc
    p = group(6)
    gr_ref[...] = (p * jax.nn.sigmoid(p)).astype(gr_ref.dtype)
    ga_ref[...] = jax.nn.sigmoid(group(7)).astype(ga_ref.dtype)
    gb_ref[...] = jax.nn.sigmoid(group(8)).astype(gb_ref.dtype)
    vr_ref[...] = group(5).astype(vr_ref.dtype)


def _proj(x, mod, mod_kind, rows_per_mod, norm_g, w_in, tables, n_tab_tiles, transposed, batch, seq):
    rows = x.shape[0]
    tm = min(PROJ_TM, rows)
    row = lambda i: (i, 0)
    const = lambda i: (0, 0)
    tab = lambda w: pl.BlockSpec((tm, w), lambda i: (i % n_tab_tiles, 0))
    rowspec = pl.BlockSpec((tm, D_MODEL), row)
    if transposed:
        tiles = seq // tm
        tspec = pl.BlockSpec((None, D_MODEL, tm), lambda i: (i // tiles, 0, i % tiles))
        t_shape = lambda dt: jax.ShapeDtypeStruct((batch, D_MODEL, seq), dt)
        r_shape = lambda dt: jax.ShapeDtypeStruct((rows, D_MODEL), dt)
        vspec = pl.BlockSpec((tm, ATTN_V_WIDTH), row)
        v_shape = jax.ShapeDtypeStruct((rows, ATTN_V_WIDTH), BF16)
        out_specs = [rowspec, tspec, tspec, rowspec, vspec, rowspec, tspec, rowspec, rowspec, rowspec, rowspec]
        out_shape = [r_shape(BF16), t_shape(F32), t_shape(BF16), r_shape(F32), v_shape, r_shape(BF16),
                     t_shape(BF16), r_shape(BF16), r_shape(BF16), r_shape(BF16), r_shape(BF16)]
    else:
        out_specs = [rowspec] * N_PROJ
        out_shape = [jax.ShapeDtypeStruct((rows, D_MODEL), F32)] * N_PROJ
    return pl.pallas_call(
        functools.partial(_proj_kernel, transposed=transposed),
        grid=(rows // tm,),
        in_specs=[
            rowspec,
            _mod_spec(mod_kind, tm, rows_per_mod, 3),
            _mod_spec(mod_kind, tm, rows_per_mod, 4),
            pl.BlockSpec((1, D_MODEL), const),
            pl.BlockSpec((D_MODEL, N_PROJ * D_MODEL), const, pipeline_mode=pl.Buffered(1)),
            tab(LANES), tab(LANES), tab(RET_DK), tab(RET_DK),
        ],
        out_specs=out_specs,
        out_shape=out_shape,
        compiler_params=_cparams(("parallel",)),
        name="mixer_proj",
    )(x, mod, mod, norm_g, w_in, *tables)


def _lambda(lq1, lk1, lq2, lk2, lam_init):
    return (jnp.exp(jnp.sum(lq1 * lk1, axis=-1, keepdims=True))
            - jnp.exp(jnp.sum(lq2 * lk2, axis=-1, keepdims=True)) + lam_init)


def _attn_kernel(qi_ref, kj_ref, q_ref, kt_ref, v_ref, lq1_ref, lk1_ref, lq2_ref, lk2_ref, sg_ref, o_ref,
                 qz_ref, m_ref, acc_ref, *, tq, lam_init):
    step = pl.program_id(1)
    qi = qi_ref[step]
    kj = kj_ref[step]
    lane = lax.broadcasted_iota(jnp.int32, (tq, DA_DV), 1)
    n_rep = tq // LANES

    def block(masked, first):
        if masked:
            r = lax.broadcasted_iota(jnp.int32, (2 * tq, tq), 0) % tq
            c = lax.broadcasted_iota(jnp.int32, (2 * tq, tq), 1)
            keep = c <= r
            lam = _lambda(lq1_ref[...], lk1_ref[...], lq2_ref[...], lk2_ref[...], lam_init)

        def scores(h):
            if first:
                qh = q_ref[:, h * DA_DV:(h + 1) * DA_DV]
                qz_ref[h, :tq, :] = jnp.where(lane < DA_DK, qh, jnp.zeros_like(qh))
                qz_ref[h, tq:, :] = jnp.where(lane >= DA_DK, qh, jnp.zeros_like(qh))
            return jnp.dot(qz_ref[h], kt_ref[h * DA_DV:(h + 1) * DA_DV, :], preferred_element_type=F32)

        s_next = scores(0)
        for h in range(DA_HEADS):
            s = s_next
            if h + 1 < DA_HEADS:
                s_next = scores(h + 1)
            if masked:
                s = jnp.where(keep, s, NEG_INF)
            row_max = jnp.max(s, axis=-1, keepdims=True)
            if first:
                m_new = jnp.broadcast_to(row_max, (2 * tq, LANES))
            else:
                m_old = m_ref[h]
                m_new = jnp.maximum(m_old, row_max)
            p = jnp.exp2(s - jnp.concatenate([m_new] * n_rep, axis=1))
            pv = jnp.dot(p.astype(BF16), v_ref[:, h * 2 * DA_DV:(h + 1) * 2 * DA_DV], preferred_element_type=F32)
            if first:
                acc = pv
            else:
                alpha = jnp.exp2(m_old - m_new)
                acc = jnp.concatenate([alpha, alpha], axis=1) * acc_ref[h] + pv
            if masked:
                o = acc[:, :DA_DV] / acc[:, DA_DV:]
                d = o[:tq] - lam * o[tq:]
                o_ref[:, h * DA_DV:(h + 1) * DA_DV] = (
                    _rms(d, sg_ref[...], SUBLN_EPS) * (1.0 - lam_init)).astype(o_ref.dtype)
            else:
                acc_ref[h] = acc
                m_ref[h] = m_new

    for masked in (False, True):
        for first in (False, True):
            @pl.when(((kj == qi) if masked else (kj < qi)) & ((kj == 0) if first else (kj > 0)))
            def _(masked=masked, first=first):
                block(masked, first)


def _attn_prompt(q, kt, v, lam_vecs, subln_g, batch, seq, lam_init):
    tq = min(ATTN_TQ, seq)
    nq = seq // tq
    pairs = [(i, j) for i in range(nq) for j in range(i + 1)]
    qi = jnp.asarray([p[0] for p in pairs], jnp.int32)
    kj = jnp.asarray([p[1] for p in pairs], jnp.int32)
    vec = lambda w: pl.BlockSpec((1, w), lambda b, s, qi, kj: (0, 0))
    grid_spec = pltpu.PrefetchScalarGridSpec(
        num_scalar_prefetch=2,
        grid=(batch, len(pairs)),
        in_specs=[
            pl.BlockSpec((tq, D_MODEL), lambda b, s, qi, kj: (b * nq + qi[s], 0)),
            pl.BlockSpec((None, D_MODEL, tq), lambda b, s, qi, kj: (b, 0, kj[s])),
            pl.BlockSpec((tq, ATTN_V_WIDTH), lambda b, s, qi, kj: (b * nq + kj[s], 0)),
            vec(DA_DK), vec(DA_DK), vec(DA_DK), vec(DA_DK), vec(DA_DV),
        ],
        out_specs=pl.BlockSpec((tq, D_MODEL), lambda b, s, qi, kj: (b * nq + qi[s], 0)),
        scratch_shapes=[
            pltpu.VMEM((DA_HEADS, 2 * tq, DA_DV), BF16),
            pltpu.VMEM((DA_HEADS, 2 * tq, LANES), F32),
            pltpu.VMEM((DA_HEADS, 2 * tq, 2 * DA_DV), F32),
        ],
    )
    return pl.pallas_call(
        functools.partial(_attn_kernel, tq=tq, lam_init=lam_init),
        grid_spec=grid_spec,
        out_shape=jax.ShapeDtypeStruct((batch * seq, D_MODEL), BF16),
        compiler_params=_cparams(("parallel", "arbitrary")),
        name="diff_attn_prompt",
    )(qi, kj, q, kt, v, *lam_vecs, subln_g)


def _attn_sample_kernel(pt_ref, *refs, n_pages, page, dec_seq, lam_init):
    k_refs = refs[:n_pages]
    v_refs = refs[n_pages:2 * n_pages]
    (q_ref, kn_ref, vn_ref, lq1_ref, lk1_ref, lq2_ref, lk2_ref, sg_ref, o_ref, s_ref) = refs[2 * n_pages:]
    del pt_ref
    rows = 2 * dec_seq
    q = q_ref[...].astype(BF16)
    for i in range(n_pages):
        s_ref[:, :, i * page:(i + 1) * page] = jnp.einsum(
            "hrk,hkt->hrt", q, k_refs[i][...].astype(BF16), preferred_element_type=F32)
    qf = q.astype(F32)
    kn = kn_ref[...].astype(BF16).astype(F32)
    vn = vn_ref[...].astype(BF16).astype(F32)
    r = lax.broadcasted_iota(jnp.int32, (DA_HEADS, rows, 1), 1) % dec_seq
    s_new = [jnp.where(r >= t, jnp.sum(qf * kn[:, t:t + 1, :], axis=-1, keepdims=True), NEG_INF)
             for t in range(dec_seq)]
    s_old = s_ref[...]
    m = jnp.max(s_old, axis=-1, keepdims=True)
    for s_t in s_new:
        m = jnp.maximum(m, s_t)
    p_old = jnp.exp2(s_old - m)
    denom = jnp.sum(p_old, axis=-1, keepdims=True)
    acc_new = jnp.zeros((DA_HEADS, rows, DA_DV), F32)
    for t, s_t in enumerate(s_new):
        p_t = jnp.exp2(s_t - m)
        denom = denom + p_t
        acc_new = acc_new + p_t.astype(BF16).astype(F32) * vn[:, t:t + 1, :]
    s_ref[...] = p_old
    lam = _lambda(lq1_ref[...], lk1_ref[...], lq2_ref[...], lk2_ref[...], lam_init)
    for h in range(DA_HEADS):
        acc = acc_new[h]
        for i in range(n_pages):
            acc = acc + jnp.dot(s_ref[h, :, i * page:(i + 1) * page].astype(BF16),
                                v_refs[i][pl.ds(h, page, stride=DA_HEADS), :].astype(BF16),
                                preferred_element_type=F32)
        o = acc / denom[h]
        d = o[:dec_seq] - lam * o[dec_seq:]
        o_ref[:, h * DA_DV:(h + 1) * DA_DV] = _rms(d, sg_ref[...], SUBLN_EPS) * (1.0 - lam_init)


def _attn_sample(page_table, cache_kt, cache_v, q_bd, k_new, v_new, lam_vecs, subln_g, lam_init):
    dec_batch, n_pages = page_table.shape
    page = cache_v.shape[1]
    dec_seq = k_new.shape[2]
    kspec = lambda i: pl.BlockSpec((None, DA_HEADS, DA_DV, page), lambda b, pt: (pt[b, i], 0, 0, 0))
    cache_v = cache_v.reshape(cache_v.shape[0], page * DA_HEADS, DA_DV)
    vspec = lambda i: pl.BlockSpec((None, page * DA_HEADS, DA_DV), lambda b, pt: (pt[b, i], 0, 0))
    per_b = lambda r: pl.BlockSpec((None, DA_HEADS, r, DA_DV), lambda b, pt: (b, 0, 0, 0))
    vec = lambda w: pl.BlockSpec((1, w), lambda b, pt: (0, 0))
    grid_spec = pltpu.PrefetchScalarGridSpec(
        num_scalar_prefetch=1,
        grid=(dec_batch,),
        in_specs=[kspec(i) for i in range(n_pages)] + [vspec(i) for i in range(n_pages)] + [
            per_b(2 * dec_seq), per_b(dec_seq), per_b(dec_seq),
            vec(DA_DK), vec(DA_DK), vec(DA_DK), vec(DA_DK), vec(DA_DV)],
        out_specs=pl.BlockSpec((None, dec_seq, D_MODEL), lambda b, pt: (b, 0, 0)),
        scratch_shapes=[pltpu.VMEM((DA_HEADS, 2 * dec_seq, n_pages * page), F32)],
    )
    return pl.pallas_call(
        functools.partial(_attn_sample_kernel, n_pages=n_pages, page=page, dec_seq=dec_seq, lam_init=lam_init),
        grid_spec=grid_spec,
        out_shape=jax.ShapeDtypeStruct((dec_batch, dec_seq, D_MODEL), F32),
        compiler_params=_cparams(("parallel",)),
        name="diff_attn_sample",
    )(page_table, *([cache_kt] * n_pages), *([cache_v] * n_pages), q_bd, k_new, v_new, *lam_vecs, subln_g)


def _group_norm(o, g):
    mu = jnp.mean(o, axis=-1, keepdims=True)
    d = o - mu
    return d * lax.rsqrt(jnp.mean(d * d, axis=-1, keepdims=True) + GN_EPS) * g


def _ret_prompt_kernel(lg_ref, q_ref, kt_ref, v_ref, g_ref, o_ref, st_ref, decay_ref, *, chunk):
    c = pl.program_id(0)
    batch = q_ref.shape[0]

    @pl.when(c == 0)
    def _():
        st_ref[...] = jnp.zeros(st_ref.shape, F32)
        ri = lax.broadcasted_iota(jnp.int32, (chunk, chunk), 0)
        ci = lax.broadcasted_iota(jnp.int32, (chunk, chunk), 1)
        dist = (ri - ci).astype(F32)
        for h in range(RET_HEADS):
            decay_ref[h] = jnp.where(dist >= 0, jnp.exp(jnp.maximum(dist, 0.0) * lg_ref[h]), 0.0)

    row = lax.broadcasted_iota(jnp.int32, (chunk, 1), 0)
    col = lax.broadcasted_iota(jnp.int32, (1, chunk), 1)
    for h in range(RET_HEADS):
        lg = lg_ref[h]
        sl = slice(h * RET_DK, (h + 1) * RET_DK)
        q_decay = jnp.exp((row + 1).astype(F32) * lg)
        k_decay = jnp.exp((chunk - 1 - col).astype(F32) * lg)
        carry = jnp.exp(jnp.full((1, 1), chunk, F32) * lg)
        for b in range(batch):
            q = q_ref[b, :, sl]
            kt = kt_ref[b, sl, :]
            v = v_ref[b, :, sl]
            scores = jnp.dot(q, kt, preferred_element_type=F32) * decay_ref[h]
            inner = jnp.dot(scores.astype(BF16), v, preferred_element_type=F32)
            state = st_ref[b, h]
            cross = jnp.dot(q, state.astype(BF16), preferred_element_type=F32) * q_decay
            o_ref[b, :, sl] = _group_norm(inner + cross, g_ref[:, sl]).astype(o_ref.dtype)
            ktd = (kt.astype(F32) * k_decay).astype(BF16)
            st_ref[b, h] = carry * state + jnp.dot(ktd, v, preferred_element_type=F32)


def _ret_prompt(log_g, q, kt, v, gn_g, batch, seq):
    chunk = min(RET_L, seq)
    tok = pl.BlockSpec((batch, chunk, D_MODEL), lambda c: (0, c, 0))
    o_r, state = pl.pallas_call(
        functools.partial(_ret_prompt_kernel, chunk=chunk),
        grid=(seq // chunk,),
        in_specs=[
            pl.BlockSpec(memory_space=pltpu.SMEM),
            tok,
            pl.BlockSpec((batch, D_MODEL, chunk), lambda c: (0, 0, c)),
            tok,
            pl.BlockSpec((1, D_MODEL), lambda c: (0, 0)),
        ],
        out_specs=[tok, pl.BlockSpec((batch, RET_HEADS, RET_DK, RET_DV), lambda c: (0, 0, 0, 0))],
        out_shape=[jax.ShapeDtypeStruct((batch, seq, D_MODEL), BF16),
                   jax.ShapeDtypeStruct((batch, RET_HEADS, RET_DK, RET_DV), F32)],
        scratch_shapes=[pltpu.VMEM((RET_HEADS, chunk, chunk), F32)],
        compiler_params=_cparams(("arbitrary",)),
        name="retention_prompt",
    )(log_g, q.reshape(batch, seq, D_MODEL), kt, v.reshape(batch, seq, D_MODEL), gn_g)
    return o_r.reshape(batch * seq, D_MODEL), state


def _ret_sample_kernel(lg_ref, q_ref, k_ref, v_ref, s_ref, g_ref, o_ref, st_ref, *, dec_seq):
    n_b, rows = q_ref.shape[:2]
    ri = lax.broadcasted_iota(jnp.int32, (rows, 1), 0)
    zpad = jnp.zeros((LANES - rows, RET_DK), F32)
    for h in range(RET_HEADS):
        lg = lg_ref[h]
        sl = slice(h * RET_DK, (h + 1) * RET_DK)
        decay = []
        for j in range(dec_seq):
            dist = (ri - j).astype(F32)
            decay.append(jnp.where(dist >= 0, jnp.exp(jnp.maximum(dist, 0.0) * lg), 0.0))
        q_decay = jnp.exp((ri + 1).astype(F32) * lg)
        k_decay = jnp.where(ri < dec_seq, jnp.exp((dec_seq - 1 - ri).astype(F32) * lg), 0.0)
        carry = jnp.exp(jnp.full((1, 1), dec_seq, F32) * lg)
        for b in range(n_b):
            q = q_ref[b, :, sl]
            k = k_ref[b, :, sl]
            v = v_ref[b, :, sl]
            qb = q.astype(BF16).astype(F32)
            kb = k.astype(BF16).astype(F32)
            vb = v.astype(BF16).astype(F32)
            state = s_ref[b, h]
            inner = jnp.zeros((rows, RET_DV), F32)
            for j in range(dec_seq):
                score = jnp.sum(qb * kb[j:j + 1, :], axis=-1, keepdims=True) * decay[j]
                inner = inner + score.astype(BF16).astype(F32) * vb[j:j + 1, :]
            cross = jnp.dot(q.astype(BF16), state.astype(BF16), preferred_element_type=F32) * q_decay
            o_ref[b, :, sl] = _group_norm(inner + cross, g_ref[:, sl])
            kd_t = jnp.concatenate([k * k_decay, zpad], axis=0).T.astype(BF16)
            v_pad = jnp.concatenate([v, zpad], axis=0).astype(BF16)
            st_ref[b, h] = carry * state + jnp.dot(kd_t, v_pad, preferred_element_type=F32)


def _ret_sample(log_g, q, k, v, state, gn_g, dec_seq):
    dec_batch, rows = q.shape[:2]
    n_b = math.gcd(dec_batch, RET_SAMPLE_NB)
    tok = pl.BlockSpec((n_b, rows, D_MODEL), lambda b: (b, 0, 0))
    st = pl.BlockSpec((n_b, RET_HEADS, RET_DK, RET_DV), lambda b: (b, 0, 0, 0))
    return pl.pallas_call(
        functools.partial(_ret_sample_kernel, dec_seq=dec_seq),
        grid=(dec_batch // n_b,),
        in_specs=[pl.BlockSpec(memory_space=pltpu.SMEM), tok, tok, tok, st,
                  pl.BlockSpec((1, D_MODEL), lambda b: (0, 0))],
        out_specs=[tok, st],
        out_shape=[jax.ShapeDtypeStruct((dec_batch, rows, D_MODEL), F32),
                   jax.ShapeDtypeStruct(state.shape, F32)],
        compiler_params=_cparams(("parallel",)),
        name="retention_sample",
    )(log_g, q, k, v, state, gn_g)


def kernel(x_prompt, x_sample, cache_k, cache_v, state_ret, page_table, c_prompt, c_sample, ada_w, ada_b, norm_ffn1, norm_mix, norm_ffn2, ffn1_w_in, ffn1_w_out, ffn2_w_in, ffn2_w_out, w_in, w_out, lam_q1, lam_k1, lam_q2, lam_k2, subln_g, ret_norm_g, norm_final):
    batch, seq, _ = x_prompt.shape
    dec_batch, dec_seq, _ = x_sample.shape
    depth = ada_w.shape[0]
    page = cache_k.shape[2]
    past_len = page_table.shape[1] * page
    log_g = jnp.log1p(-jnp.exp2(-5.0 - jnp.arange(RET_HEADS, dtype=F32)))
    nf = norm_final[None, :]

    tab_p = _rotation_tables(seq, 0)
    sub = 8
    tab_s = [jnp.tile(t[:dec_seq], (dec_batch, 1)) for t in _rotation_tables(sub, past_len)]

    rows_s = dec_batch * dec_seq
    n_c = rows_s + batch
    c_all = jnp.concatenate([jnp.repeat(c_sample, dec_seq, axis=0), c_prompt,
                             jnp.zeros((-n_c % (2 * sub), D_MODEL), F32)], axis=0)

    yp = x_prompt.reshape(batch * seq, D_MODEL)
    ys = x_sample.reshape(dec_batch * dec_seq, D_MODEL)
    kp_l, vp_l, sp_l, ks_l, vs_l, ss_l = [], [], [], [], [], []
    for l in range(depth):
        lam_init = 0.8 - 0.6 * math.exp(-0.3 * l)
        mod_s = _adaln(c_all, ada_w[l], ada_b[l][None, :])
        mod_p = mod_s[rows_s:n_c].reshape(batch, 1, N_MOD * D_MODEL)
        w1i, w1o = ffn1_w_in[l].astype(BF16), ffn1_w_out[l].astype(BF16)
        w2i, w2o = ffn2_w_in[l].astype(BF16), ffn2_w_out[l].astype(BF16)
        wi, wo = w_in[l].astype(BF16), w_out[l].astype(BF16)
        g1, gm, g2 = norm_ffn1[l][None, :], norm_mix[l][None, :], norm_ffn2[l][None, :]
        lam_vecs = (lam_q1[l][None, :], lam_k1[l][None, :], lam_q2[l][None, :], lam_k2[l][None, :])
        sg = subln_g[l][None, :]
        gn = ret_norm_g[l][None, :]
        last = l == depth - 1

        x1 = _ffn(yp, mod_p, "batch", seq, (0, 1, 2), g1, w1i, w1o, nf, False)
        (q, ktf, ktb, vf, vb, qr, krt, vr, gr, ga, gb) = _proj(
            x1, mod_p, "batch", seq, gm, wi, tab_p, seq // min(PROJ_TM, seq), True, batch, seq)
        o_a = _attn_prompt(q, ktb, vb, lam_vecs, sg, batch, seq, lam_init)
        o_r, sp = _ret_prompt(log_g, qr, krt, vr, gn, batch, seq)
        yp = _ffn(x1, mod_p, "batch", seq, (6, 7, 8), g2, w2i, w2o, nf, last, mixer=(5, o_a, o_r, gr, ga, gb, wo))
        kp_l.append(jnp.transpose(ktf.reshape(batch, 2 * DA_HEADS, DA_DK, seq), (0, 3, 1, 2)))
        vp_l.append(vf.reshape(batch, seq, DA_HEADS, DA_DV))
        sp_l.append(sp)

        x1 = _ffn(ys, mod_s, "token", 1, (0, 1, 2), g1, w1i, w1o, nf, False)
        (q, k, v, qr, kr, vr, gr, ga, gb) = _proj(
            x1, mod_s, "token", 1, gm, wi, tab_s, rows_s // min(PROJ_TM, rows_s), False, dec_batch, dec_seq)
        qh = q.reshape(dec_batch, dec_seq, DA_HEADS, 2, DA_DK).transpose(0, 2, 3, 1, 4)
        eye = jnp.eye(2, dtype=F32)
        q_bd = (qh[:, :, :, :, None, :] * eye[None, None, :, None, :, None]).reshape(
            dec_batch, DA_HEADS, 2 * dec_seq, DA_DV)
        to_heads = lambda a: a.reshape(dec_batch, dec_seq, DA_HEADS, DA_DV).transpose(0, 2, 1, 3)
        cache_kt = jnp.transpose(cache_k[l], (0, 2, 3, 1)).reshape(-1, DA_HEADS, DA_DV, page)
        o_a = _attn_sample(page_table, cache_kt, cache_v[l], q_bd, to_heads(k), to_heads(v), lam_vecs, sg, lam_init)
        pad = lambda a: jnp.pad(a.reshape(dec_batch, dec_seq, D_MODEL), ((0, 0), (0, sub - dec_seq), (0, 0)))
        o_r, ssm = _ret_sample(log_g, pad(qr), pad(kr), pad(vr), state_ret[l], gn, dec_seq)
        o_r = o_r[:, :dec_seq].reshape(rows_s, D_MODEL)
        ys = _ffn(x1, mod_s, "token", 1, (6, 7, 8), g2, w2i, w2o, nf, last,
                  mixer=(5, o_a.reshape(rows_s, D_MODEL), o_r, gr, ga, gb, wo))
        ks_l.append(k.reshape(dec_batch, dec_seq, 2 * DA_HEADS, DA_DK))
        vs_l.append(v.reshape(dec_batch, dec_seq, DA_HEADS, DA_DV))
        ss_l.append(ssm)

    return (yp.reshape(batch, seq, D_MODEL), ys.reshape(dec_batch, dec_seq, D_MODEL),
            jnp.stack(kp_l), jnp.stack(vp_l), jnp.stack(sp_l), jnp.stack(ks_l), jnp.stack(vs_l), jnp.stack(ss_l))
```

```python
import functools
import math

import jax
import jax.numpy as jnp
from jax import lax
from jax.experimental import pallas as pl
from jax.experimental.pallas import tpu as pltpu

F32 = jnp.float32
BF16 = jnp.bfloat16

D_MODEL = 1024
DA_DK = 64
DA_DV = 128
DA_HEADS = 8
RET_HEADS = 4
RET_DK = 256
RET_DV = 256
D_FF = 2816
N_MOD = 9
N_PROJ = 9
HALF_STEP = 0.5
ROPE_THETA = 10000.0
RET_THETA = 10000.0
NORM_EPS = 1e-6
SUBLN_EPS = 1e-5
GN_EPS = 1e-5
NEG_INF = -1e30
LOG2E = 1.4426950408889634
LANES = 128
VMEM_LIMIT = 56 * 1024 * 1024

FFN_TM = 512
FFN_TF = 256
PROJ_TM = 256
ATTN_TQ = 512
ATTN_V_WIDTH = 2 * DA_HEADS * DA_DV
RET_L = 512
RET_SAMPLE_NB = 8


def _cparams(sem):
    return pltpu.CompilerParams(dimension_semantics=sem, vmem_limit_bytes=VMEM_LIMIT)


def _rms(x, g, eps):
    return x * lax.rsqrt(jnp.mean(x * x, axis=-1, keepdims=True) + eps) * g


def _mod_spec(kind, tm, rows_per_mod, piece):
    if kind == "batch":
        return pl.BlockSpec((None, 1, D_MODEL), lambda i, *_: ((i * tm) // rows_per_mod, 0, piece))
    return pl.BlockSpec((tm, D_MODEL), lambda i, *_: (i, piece))


def _adaln_kernel(c_ref, w_ref, b_ref, o_ref):
    c = c_ref[...]
    a = (c * jax.nn.sigmoid(c)).astype(BF16)
    o_ref[...] = jnp.dot(a, w_ref[...].astype(BF16), preferred_element_type=F32) + b_ref[...]


def _adaln(c, w, b):
    m = c.shape[0]
    n = w.shape[1]
    tn = 1536
    return pl.pallas_call(
        _adaln_kernel,
        grid=(n // tn,),
        in_specs=[
            pl.BlockSpec((m, D_MODEL), lambda j: (0, 0)),
            pl.BlockSpec((D_MODEL, tn), lambda j: (0, j)),
            pl.BlockSpec((1, tn), lambda j: (0, j)),
        ],
        out_specs=pl.BlockSpec((m, tn), lambda j: (0, j)),
        out_shape=jax.ShapeDtypeStruct((m, n), F32),
        compiler_params=_cparams(("arbitrary",)),
        name="adaln",
    )(c, w, b)


def _merged_mixer(gtm_ref, oa_ref, or_ref, gr_ref, ga_ref, gb_ref, wm_ref):
    o_r = or_ref[...].astype(F32) * gr_ref[...].astype(F32)
    merged = ga_ref[...].astype(F32) * oa_ref[...].astype(F32) + gb_ref[...].astype(F32) * o_r
    return gtm_ref[...] * jnp.dot(merged.astype(BF16), wm_ref[...], preferred_element_type=F32)


def _ffn_kernel(*refs, final_norm, merge, n_main):
    if merge:
        (x_ref, gtm_ref, oa_ref, or_ref, gr_ref, ga_ref, gb_ref, wm_ref), refs = refs[:8], refs[8:]
    elif n_main is not None:
        (x_ref, xs_ref), refs = refs[:2], refs[2:]
    else:
        x_ref, refs = refs[0], refs[1:]
    if n_main is not None:
        (sh_ref, sc_ref, gt_ref, shs_ref, scs_ref, gts_ref, g_ref, wi_ref, wo_ref, nf_ref, o_ref, os_ref) = refs
        tail = pl.program_id(0) >= n_main
        pick = lambda a_ref, b_ref: jnp.where(tail, b_ref[...], a_ref[...])
        x, sh, sc, gt = pick(x_ref, xs_ref), pick(sh_ref, shs_ref), pick(sc_ref, scs_ref), pick(gt_ref, gts_ref)
    else:
        sh_ref, sc_ref, gt_ref, g_ref, wi_ref, wo_ref, nf_ref, o_ref = refs
        x, sh, sc, gt = x_ref[...], sh_ref[...], sc_ref[...], gt_ref[...]
    if merge:
        x = x + _merged_mixer(gtm_ref, oa_ref, or_ref, gr_ref, ga_ref, gb_ref, wm_ref)
    h = (_rms(x, g_ref[...], NORM_EPS) * (1.0 + sc) + sh).astype(BF16)
    acc = None
    for c in range(D_FF // FFN_TF):
        lo = c * FFN_TF
        a = jnp.dot(h, wi_ref[:, lo:lo + FFN_TF], preferred_element_type=F32)
        b = jnp.dot(h, wi_ref[:, D_FF + lo:D_FF + lo + FFN_TF], preferred_element_type=F32)
        act = (a * jax.nn.sigmoid(a) * b).astype(BF16)
        part = jnp.dot(act, wo_ref[lo:lo + FFN_TF, :], preferred_element_type=F32)
        acc = part if acc is None else acc + part
    out = x + HALF_STEP * gt * acc
    if final_norm:
        out = _rms(out, nf_ref[...], NORM_EPS)
    if n_main is None:
        o_ref[...] = out
    else:
        @pl.when(jnp.logical_not(tail))
        def _():
            o_ref[...] = out

        @pl.when(tail)
        def _():
            os_ref[...] = out


def _ffn_two_groups(x, mod, rows_per_mod, x_tok, mod_tok, pieces, norm_g, w_in, w_out, norm_final):
    tm = FFN_TM
    n_main, n_tail = x.shape[0] // tm, x_tok.shape[0] // tm
    const = lambda i: (0, 0)
    main = lambda i: jnp.minimum(i, n_main - 1)
    tail = lambda i: jnp.maximum(i - n_main, 0)
    resident = lambda shape: pl.BlockSpec(shape, const, pipeline_mode=pl.Buffered(1))
    main_rows = pl.BlockSpec((tm, D_MODEL), lambda i: (main(i), 0))
    tail_rows = pl.BlockSpec((tm, D_MODEL), lambda i: (tail(i), 0))
    seq_mod = lambda p: pl.BlockSpec((None, 1, D_MODEL), lambda i: ((main(i) * tm) // rows_per_mod, 0, p))
    tok_mod = lambda p: pl.BlockSpec((tm, D_MODEL), lambda i: (tail(i), p))
    return pl.pallas_call(
        functools.partial(_ffn_kernel, final_norm=False, merge=False, n_main=n_main),
        grid=(n_main + n_tail,),
        in_specs=[main_rows, tail_rows, *[seq_mod(p) for p in pieces], *[tok_mod(p) for p in pieces],
                  pl.BlockSpec((1, D_MODEL), const), resident((D_MODEL, 2 * D_FF)), resident((D_FF, D_MODEL)),
                  pl.BlockSpec((1, D_MODEL), const)],
        out_specs=[main_rows, tail_rows],
        out_shape=[jax.ShapeDtypeStruct(x.shape, F32), jax.ShapeDtypeStruct(x_tok.shape, F32)],
        compiler_params=_cparams(("arbitrary",)),
        name="ffn_two_groups",
    )(x, x_tok, mod, mod, mod, mod_tok, mod_tok, mod_tok, norm_g, w_in, w_out, norm_final)


def _ffn(x, mod, mod_kind, rows_per_mod, pieces, norm_g, w_in, w_out, norm_final, final_norm, mixer=None):
    rows = x.shape[0]
    tm = min(FFN_TM, rows)
    if mixer is not None and mixer[1].dtype == F32:
        tm = tm // 2
    const = lambda i: (0, 0)
    rowspec = pl.BlockSpec((tm, D_MODEL), lambda i: (i, 0))
    resident = lambda shape: pl.BlockSpec(shape, const, pipeline_mode=pl.Buffered(1))
    in_specs = [rowspec]
    args = [x]
    if mixer is not None:
        in_specs += [_mod_spec(mod_kind, tm, rows_per_mod, mixer[0])] + [rowspec] * 5 + [
            resident((D_MODEL, D_MODEL))]
        args += [mod, *mixer[1:]]
    in_specs += [
        _mod_spec(mod_kind, tm, rows_per_mod, pieces[0]),
        _mod_spec(mod_kind, tm, rows_per_mod, pieces[1]),
        _mod_spec(mod_kind, tm, rows_per_mod, pieces[2]),
        pl.BlockSpec((1, D_MODEL), const),
        resident((D_MODEL, 2 * D_FF)),
        resident((D_FF, D_MODEL)),
        pl.BlockSpec((1, D_MODEL), const),
    ]
    args += [mod, mod, mod, norm_g, w_in, w_out, norm_final]
    return pl.pallas_call(
        functools.partial(_ffn_kernel, final_norm=final_norm, merge=mixer is not None, n_main=None),
        grid=(rows // tm,),
        in_specs=in_specs,
        out_specs=rowspec,
        out_shape=jax.ShapeDtypeStruct((rows, D_MODEL), F32),
        compiler_params=_cparams(("parallel",)),
        name="ffn",
    )(*args)


def _table_kernel(inv_r_ref, sgn_r_ref, inv_t_ref, sgn_t_ref, cr_ref, sr_ref, ct_ref, st_ref,
                  lcr_ref, lsr_ref, lct_ref, lst_ref, *, offset):
    rows = cr_ref.shape[0]
    i = pl.program_id(0)

    @pl.when(i == 0)
    def _():
        local = lax.broadcasted_iota(jnp.int32, (rows, 1), 0).astype(F32)
        ang = local * inv_r_ref[...]
        lcr_ref[...] = jnp.cos(ang)
        lsr_ref[...] = jnp.sin(ang)
        ang = local * inv_t_ref[...]
        lct_ref[...] = jnp.cos(ang)
        lst_ref[...] = jnp.sin(ang)

    base = jnp.full((1, 1), i * rows + offset, jnp.int32).astype(F32)
    for inv_ref, sgn_ref, lc_ref, ls_ref, c_ref, s_ref in (
            (inv_r_ref, sgn_r_ref, lcr_ref, lsr_ref, cr_ref, sr_ref),
            (inv_t_ref, sgn_t_ref, lct_ref, lst_ref, ct_ref, st_ref)):
        ang = base * inv_ref[...]
        cb, sb = jnp.cos(ang), jnp.sin(ang)
        lc, ls = lc_ref[...], ls_ref[...]
        c_ref[...] = lc * cb - ls * sb
        s_ref[...] = (ls * cb + lc * sb) * sgn_ref[...]


def _rotation_tables(n_pos, offset):
    lane = jnp.arange(LANES)
    half = DA_DK // 2
    inv_r = (ROPE_THETA ** (-((lane % DA_DK) % half).astype(F32) / half))[None, :]
    sgn_r = jnp.where((lane % DA_DK) < half, -1.0, 1.0).astype(F32)[None, :]
    lane_t = jnp.arange(RET_DK)
    angle = 1.0 / (RET_THETA ** jnp.linspace(0.0, 1.0, RET_DK // 2, dtype=F32))
    inv_t = angle[lane_t // 2][None, :]
    sgn_t = jnp.where(lane_t % 2 == 0, -1.0, 1.0).astype(F32)[None, :]
    rows = min(n_pos, 512)
    vec = lambda w: pl.BlockSpec((1, w), lambda i: (0, 0))
    tab = lambda w: pl.BlockSpec((rows, w), lambda i: (i, 0))
    return pl.pallas_call(
        functools.partial(_table_kernel, offset=offset),
        grid=(n_pos // rows,),
        in_specs=[vec(LANES), vec(LANES), vec(RET_DK), vec(RET_DK)],
        out_specs=[tab(LANES), tab(LANES), tab(RET_DK), tab(RET_DK)],
        out_shape=[jax.ShapeDtypeStruct((n_pos, LANES), F32), jax.ShapeDtypeStruct((n_pos, LANES), F32),
                   jax.ShapeDtypeStruct((n_pos, RET_DK), F32), jax.ShapeDtypeStruct((n_pos, RET_DK), F32)],
        scratch_shapes=[pltpu.VMEM((rows, LANES), F32), pltpu.VMEM((rows, LANES), F32),
                        pltpu.VMEM((rows, RET_DK), F32), pltpu.VMEM((rows, RET_DK), F32)],
        compiler_params=_cparams(("arbitrary",)),
        name="rotation_tables",
    )(inv_r, sgn_r, inv_t, sgn_t)


def _pair_rotate(x, cos, sin_signed, shift):
    lane = lax.broadcasted_iota(jnp.int32, x.shape, 1)
    partner = jnp.where((lane % (2 * shift)) < shift, pltpu.roll(x, LANES - shift, 1), pltpu.roll(x, shift, 1))
    return x * cos + partner * sin_signed


def _proj_kernel(x_ref, sh_ref, sc_ref, g_ref, w_ref, cr_ref, sr_ref, ct_ref, st_ref, *out_refs, transposed):
    h = (_rms(x_ref[...], g_ref[...], NORM_EPS) * (1.0 + sc_ref[...]) + sh_ref[...]).astype(BF16)
    cr, sr = cr_ref[...], sr_ref[...]
    n_chunks = D_MODEL // LANES
    q_scale = (DA_DK ** -0.5) * LOG2E
    k_scale = RET_DK ** -0.5

    def group(gi):
        return jnp.dot(h, w_ref[:, gi * D_MODEL:(gi + 1) * D_MODEL], preferred_element_type=F32)

    def chunk(p, c):
        return p[:, c * LANES:(c + 1) * LANES]

    def ret_tab(t_ref, c):
        half = (c % (RET_DK // LANES)) * LANES
        return t_ref[:, half:half + LANES]

    if transposed:
        (q_ref, ktf_ref, ktb_ref, vf_ref, vb_ref, qr_ref, krt_ref, vr_ref, gr_ref, ga_ref, gb_ref) = out_refs
    else:
        (q_ref, k_ref, v_ref, qr_ref, kr_ref, vr_ref, gr_ref, ga_ref, gb_ref) = out_refs

    p = group(0)
    for c in range(n_chunks):
        q_ref[:, c * LANES:(c + 1) * LANES] = (_pair_rotate(chunk(p, c), cr, sr, DA_DK // 2) * q_scale).astype(q_ref.dtype)
    p = group(1)
    for c in range(n_chunks):
        kc = _pair_rotate(chunk(p, c), cr, sr, DA_DK // 2)
        if transposed:
            kt = kc.T
            ktf_ref[c * LANES:(c + 1) * LANES, :] = kt
            ktb_ref[c * LANES:(c + 1) * LANES, :] = kt.astype(BF16)
        else:
            k_ref[:, c * LANES:(c + 1) * LANES] = kc
    p = group(2)
    if transposed:
        vf_ref[...] = p
        ones = jnp.ones((p.shape[0], DA_DV), BF16)
        for hd in range(DA_HEADS):
            vb_ref[:, 2 * hd * DA_DV:(2 * hd + 1) * DA_DV] = chunk(p, hd).astype(BF16)
            vb_ref[:, (2 * hd + 1) * DA_DV:(2 * hd + 2) * DA_DV] = ones
    else:
        v_ref[...] = p
    p = group(3)
    for c in range(n_chunks):
        qc = _pair_rotate(chunk(p, c), ret_tab(ct_ref, c), ret_tab(st_ref, c), 1)
        qr_ref[:, c * LANES:(c + 1) * LANES] = qc.astype(qr_ref.dtype)
    p = group(4)
    for c in range(n_chunks):
        kc = _pair_rotate(chunk(p, c), ret_tab(ct_ref, c), ret_tab(st_ref, c), 1) * k_scale
        if transposed:
            krt_ref[c * LANES:(c + 1) * LANES, :] = kc.T.astype(BF16)
        else:
            kr_ref[:, c * LANES:(c + 1) * LANES] = kc
    vr_ref[...] = group(5).astype(vr_ref.dtype)
    p = group(6)
    gr_ref[...] = (p * jax.nn.sigmoid(p)).astype(gr_ref.dtype)
    ga_ref[...] = jax.nn.sigmoid(group(7)).astype(ga_ref.dtype)
    gb_ref[...] = jax.nn.sigmoid(group(8)).astype(gb_ref.dtype)


def _proj(x, mod, mod_kind, rows_per_mod, norm_g, w_in, tables, n_tab_tiles, transposed, batch, seq):
    rows = x.shape[0]
    tm = min(PROJ_TM, rows)
    row = lambda i: (i, 0)
    const = lambda i: (0, 0)
    tab = lambda w: pl.BlockSpec((tm, w), lambda i: (i % n_tab_tiles, 0))
    rowspec = pl.BlockSpec((tm, D_MODEL), row)
    if transposed:
        tiles = seq // tm
        tspec = pl.BlockSpec((None, D_MODEL, tm), lambda i: (i // tiles, 0, i % tiles))
        t_shape = lambda dt: jax.ShapeDtypeStruct((batch, D_MODEL, seq), dt)
        r_shape = lambda dt: jax.ShapeDtypeStruct((rows, D_MODEL), dt)
        vspec = pl.BlockSpec((tm, ATTN_V_WIDTH), row)
        v_shape = jax.ShapeDtypeStruct((rows, ATTN_V_WIDTH), BF16)
        out_specs = [rowspec, tspec, tspec, rowspec, vspec, rowspec, tspec, rowspec, rowspec, rowspec, rowspec]
        out_shape = [r_shape(BF16), t_shape(F32), t_shape(BF16), r_shape(F32), v_shape, r_shape(BF16),
                     t_shape(BF16), r_shape(BF16), r_shape(BF16), r_shape(BF16), r_shape(BF16)]
    else:
        out_specs = [rowspec] * N_PROJ
        out_shape = [jax.ShapeDtypeStruct((rows, D_MODEL), F32)] * N_PROJ
    return pl.pallas_call(
        functools.partial(_proj_kernel, transposed=transposed),
        grid=(rows // tm,),
        in_specs=[
            rowspec,
            _mod_spec(mod_kind, tm, rows_per_mod, 3),
            _mod_spec(mod_kind, tm, rows_per_mod, 4),
            pl.BlockSpec((1, D_MODEL), const),
            pl.BlockSpec((D_MODEL, N_PROJ * D_MODEL), const, pipeline_mode=pl.Buffered(1)),
            tab(LANES), tab(LANES), tab(RET_DK), tab(RET_DK),
        ],
        out_specs=out_specs,
        out_shape=out_shape,
        compiler_params=_cparams(("parallel",)),
        name="mixer_proj",
    )(x, mod, mod, norm_g, w_in, *tables)


def _lambda(lq1, lk1, lq2, lk2, lam_init):
    return (jnp.exp(jnp.sum(lq1 * lk1, axis=-1, keepdims=True))
            - jnp.exp(jnp.sum(lq2 * lk2, axis=-1, keepdims=True)) + lam_init)


def _attn_kernel(qi_ref, kj_ref, q_ref, kt_ref, v_ref, lq1_ref, lk1_ref, lq2_ref, lk2_ref, sg_ref, o_ref,
                 qz_ref, m_ref, acc_ref, *, tq, lam_init):
    step = pl.program_id(1)
    qi = qi_ref[step]
    kj = kj_ref[step]
    lane = lax.broadcasted_iota(jnp.int32, (tq, DA_DV), 1)
    n_rep = tq // LANES

    def block(masked, first):
        if masked:
            r = lax.broadcasted_iota(jnp.int32, (2 * tq, tq), 0) % tq
            c = lax.broadcasted_iota(jnp.int32, (2 * tq, tq), 1)
            keep = c <= r
            lam = _lambda(lq1_ref[...], lk1_ref[...], lq2_ref[...], lk2_ref[...], lam_init)

        def scores(h):
            if first:
                qh = q_ref[:, h * DA_DV:(h + 1) * DA_DV]
                qz_ref[h, :tq, :] = jnp.where(lane < DA_DK, qh, jnp.zeros_like(qh))
                qz_ref[h, tq:, :] = jnp.where(lane >= DA_DK, qh, jnp.zeros_like(qh))
            return jnp.dot(qz_ref[h], kt_ref[h * DA_DV:(h + 1) * DA_DV, :], preferred_element_type=F32)

        s_next = scores(0)
        for h in range(DA_HEADS):
            s = s_next
            if h + 1 < DA_HEADS:
                s_next = scores(h + 1)
            if masked:
                s = jnp.where(keep, s, NEG_INF)
            row_max = jnp.max(s, axis=-1, keepdims=True)
            if first:
                m_new = jnp.broadcast_to(row_max, (2 * tq, LANES))
            else:
                m_old = m_ref[h]
                m_new = jnp.maximum(m_old, row_max)
            p = jnp.exp2(s - jnp.concatenate([m_new] * n_rep, axis=1))
            pv = jnp.dot(p.astype(BF16), v_ref[:, h * 2 * DA_DV:(h + 1) * 2 * DA_DV], preferred_element_type=F32)
            if first:
                acc = pv
            else:
                alpha = jnp.exp2(m_old - m_new)
                acc = jnp.concatenate([alpha, alpha], axis=1) * acc_ref[h] + pv
            if masked:
                o = acc[:, :DA_DV] / acc[:, DA_DV:]
                d = o[:tq] - lam * o[tq:]
                o_ref[:, h * DA_DV:(h + 1) * DA_DV] = (
                    _rms(d, sg_ref[...], SUBLN_EPS) * (1.0 - lam_init)).astype(o_ref.dtype)
            else:
                acc_ref[h] = acc
                m_ref[h] = m_new

    for masked in (False, True):
        for first in (False, True):
            @pl.when(((kj == qi) if masked else (kj < qi)) & ((kj == 0) if first else (kj > 0)))
            def _(masked=masked, first=first):
                block(masked, first)


def _attn_prompt(q, kt, v, lam_vecs, subln_g, batch, seq, lam_init):
    tq = min(ATTN_TQ, seq)
    nq = seq // tq
    pairs = [(i, j) for i in range(nq) for j in range(i + 1)]
    qi = jnp.asarray([p[0] for p in pairs], jnp.int32)
    kj = jnp.asarray([p[1] for p in pairs], jnp.int32)
    vec = lambda w: pl.BlockSpec((1, w), lambda b, s, qi, kj: (0, 0))
    grid_spec = pltpu.PrefetchScalarGridSpec(
        num_scalar_prefetch=2,
        grid=(batch, len(pairs)),
        in_specs=[
            pl.BlockSpec((tq, D_MODEL), lambda b, s, qi, kj: (b * nq + qi[s], 0)),
            pl.BlockSpec((None, D_MODEL, tq), lambda b, s, qi, kj: (b, 0, kj[s])),
            pl.BlockSpec((tq, ATTN_V_WIDTH), lambda b, s, qi, kj: (b * nq + kj[s], 0)),
            vec(DA_DK), vec(DA_DK), vec(DA_DK), vec(DA_DK), vec(DA_DV),
        ],
        out_specs=pl.BlockSpec((tq, D_MODEL), lambda b, s, qi, kj: (b * nq + qi[s], 0)),
        scratch_shapes=[
            pltpu.VMEM((DA_HEADS, 2 * tq, DA_DV), BF16),
            pltpu.VMEM((DA_HEADS, 2 * tq, LANES), F32),
            pltpu.VMEM((DA_HEADS, 2 * tq, 2 * DA_DV), F32),
        ],
    )
    return pl.pallas_call(
        functools.partial(_attn_kernel, tq=tq, lam_init=lam_init),
        grid_spec=grid_spec,
        out_shape=jax.ShapeDtypeStruct((batch * seq, D_MODEL), BF16),
        compiler_params=_cparams(("parallel", "arbitrary")),
        name="diff_attn_prompt",
    )(qi, kj, q, kt, v, *lam_vecs, subln_g)


def _attn_sample_kernel(pt_ref, *refs, n_pages, page, dec_seq, lam_init):
    k_refs = refs[:n_pages]
    v_refs = refs[n_pages:2 * n_pages]
    (q_ref, kn_ref, vn_ref, lq1_ref, lk1_ref, lq2_ref, lk2_ref, sg_ref, o_ref, s_ref) = refs[2 * n_pages:]
    del pt_ref
    rows = 2 * dec_seq
    q = q_ref[...].astype(BF16)
    for i in range(n_pages):
        s_ref[:, :, i * page:(i + 1) * page] = jnp.einsum(
            "hrk,hkt->hrt", q, k_refs[i][...].astype(BF16), preferred_element_type=F32)
    qf = q.astype(F32)
    kn = kn_ref[...].astype(BF16).astype(F32)
    vn = vn_ref[...].astype(BF16).astype(F32)
    r = lax.broadcasted_iota(jnp.int32, (DA_HEADS, rows, 1), 1) % dec_seq
    s_new = [jnp.where(r >= t, jnp.sum(qf * kn[:, t:t + 1, :], axis=-1, keepdims=True), NEG_INF)
             for t in range(dec_seq)]
    s_old = s_ref[...]
    m = jnp.max(s_old, axis=-1, keepdims=True)
    for s_t in s_new:
        m = jnp.maximum(m, s_t)
    p_old = jnp.exp2(s_old - m)
    denom = jnp.sum(p_old, axis=-1, keepdims=True)
    acc_new = jnp.zeros((DA_HEADS, rows, DA_DV), F32)
    for t, s_t in enumerate(s_new):
        p_t = jnp.exp2(s_t - m)
        denom = denom + p_t
        acc_new = acc_new + p_t.astype(BF16).astype(F32) * vn[:, t:t + 1, :]
    s_ref[...] = p_old
    lam = _lambda(lq1_ref[...], lk1_ref[...], lq2_ref[...], lk2_ref[...], lam_init)
    for h in range(DA_HEADS):
        acc = acc_new[h]
        for i in range(n_pages):
            acc = acc + jnp.dot(s_ref[h, :, i * page:(i + 1) * page].astype(BF16),
                                v_refs[i][pl.ds(h, page, stride=DA_HEADS), :].astype(BF16),
                                preferred_element_type=F32)
        o = acc / denom[h]
        d = o[:dec_seq] - lam * o[dec_seq:]
        o_ref[:, h * DA_DV:(h + 1) * DA_DV] = _rms(d, sg_ref[...], SUBLN_EPS) * (1.0 - lam_init)


def _attn_sample(page_table, cache_kt, cache_v, q_bd, k_new, v_new, lam_vecs, subln_g, lam_init):
    dec_batch, n_pages = page_table.shape
    page = cache_v.shape[1]
    dec_seq = k_new.shape[2]
    kspec = lambda i: pl.BlockSpec((None, DA_HEADS, DA_DV, page), lambda b, pt: (pt[b, i], 0, 0, 0))
    cache_v = cache_v.reshape(cache_v.shape[0], page * DA_HEADS, DA_DV)
    vspec = lambda i: pl.BlockSpec((None, page * DA_HEADS, DA_DV), lambda b, pt: (pt[b, i], 0, 0))
    per_b = lambda r: pl.BlockSpec((None, DA_HEADS, r, DA_DV), lambda b, pt: (b, 0, 0, 0))
    vec = lambda w: pl.BlockSpec((1, w), lambda b, pt: (0, 0))
    grid_spec = pltpu.PrefetchScalarGridSpec(
        num_scalar_prefetch=1,
        grid=(dec_batch,),
        in_specs=[kspec(i) for i in range(n_pages)] + [vspec(i) for i in range(n_pages)] + [
            per_b(2 * dec_seq), per_b(dec_seq), per_b(dec_seq),
            vec(DA_DK), vec(DA_DK), vec(DA_DK), vec(DA_DK), vec(DA_DV)],
        out_specs=pl.BlockSpec((None, dec_seq, D_MODEL), lambda b, pt: (b, 0, 0)),
        scratch_shapes=[pltpu.VMEM((DA_HEADS, 2 * dec_seq, n_pages * page), F32)],
    )
    return pl.pallas_call(
        functools.partial(_attn_sample_kernel, n_pages=n_pages, page=page, dec_seq=dec_seq, lam_init=lam_init),
        grid_spec=grid_spec,
        out_shape=jax.ShapeDtypeStruct((dec_batch, dec_seq, D_MODEL), F32),
        compiler_params=_cparams(("parallel",)),
        name="diff_attn_sample",
    )(page_table, *([cache_kt] * n_pages), *([cache_v] * n_pages), q_bd, k_new, v_new, *lam_vecs, subln_g)


def _group_norm(o, g):
    mu = jnp.mean(o, axis=-1, keepdims=True)
    d = o - mu
    return d * lax.rsqrt(jnp.mean(d * d, axis=-1, keepdims=True) + GN_EPS) * g


def _ret_prompt_kernel(lg_ref, q_ref, kt_ref, v_ref, g_ref, o_ref, st_ref, decay_ref, *, chunk):
    c = pl.program_id(0)
    batch = q_ref.shape[0]

    @pl.when(c == 0)
    def _():
        st_ref[...] = jnp.zeros(st_ref.shape, F32)
        ri = lax.broadcasted_iota(jnp.int32, (chunk, chunk), 0)
        ci = lax.broadcasted_iota(jnp.int32, (chunk, chunk), 1)
        dist = (ri - ci).astype(F32)
        for h in range(RET_HEADS):
            decay_ref[h] = jnp.where(dist >= 0, jnp.exp(jnp.maximum(dist, 0.0) * lg_ref[h]), 0.0)

    row = lax.broadcasted_iota(jnp.int32, (chunk, 1), 0)
    col = lax.broadcasted_iota(jnp.int32, (1, chunk), 1)
    for h in range(RET_HEADS):
        lg = lg_ref[h]
        sl = slice(h * RET_DK, (h + 1) * RET_DK)
        q_decay = jnp.exp((row + 1).astype(F32) * lg)
        k_decay = jnp.exp((chunk - 1 - col).astype(F32) * lg)
        carry = jnp.exp(jnp.full((1, 1), chunk, F32) * lg)
        for b in range(batch):
            q = q_ref[b, :, sl]
            kt = kt_ref[b, sl, :]
            v = v_ref[b, :, sl]
            scores = jnp.dot(q, kt, preferred_element_type=F32) * decay_ref[h]
            inner = jnp.dot(scores.astype(BF16), v, preferred_element_type=F32)
            state = st_ref[b, h]
            cross = jnp.dot(q, state.astype(BF16), preferred_element_type=F32) * q_decay
            o_ref[b, :, sl] = _group_norm(inner + cross, g_ref[:, sl]).astype(o_ref.dtype)
            ktd = (kt.astype(F32) * k_decay).astype(BF16)
            st_ref[b, h] = carry * state + jnp.dot(ktd, v, preferred_element_type=F32)


def _ret_prompt(log_g, q, kt, v, gn_g, batch, seq):
    chunk = min(RET_L, seq)
    tok = pl.BlockSpec((batch, chunk, D_MODEL), lambda c: (0, c, 0))
    o_r, state = pl.pallas_call(
        functools.partial(_ret_prompt_kernel, chunk=chunk),
        grid=(seq // chunk,),
        in_specs=[
            pl.BlockSpec(memory_space=pltpu.SMEM),
            tok,
            pl.BlockSpec((batch, D_MODEL, chunk), lambda c: (0, 0, c)),
            tok,
            pl.BlockSpec((1, D_MODEL), lambda c: (0, 0)),
        ],
        out_specs=[tok, pl.BlockSpec((batch, RET_HEADS, RET_DK, RET_DV), lambda c: (0, 0, 0, 0))],
        out_shape=[jax.ShapeDtypeStruct((batch, seq, D_MODEL), BF16),
                   jax.ShapeDtypeStruct((batch, RET_HEADS, RET_DK, RET_DV), F32)],
        scratch_shapes=[pltpu.VMEM((RET_HEADS, chunk, chunk), F32)],
        compiler_params=_cparams(("arbitrary",)),
        name="retention_prompt",
    )(log_g, q.reshape(batch, seq, D_MODEL), kt, v.reshape(batch, seq, D_MODEL), gn_g)
    return o_r.reshape(batch * seq, D_MODEL), state


def _ret_sample_kernel(lg_ref, q_ref, k_ref, v_ref, s_ref, g_ref, o_ref, st_ref, *, dec_seq):
    n_b, rows = q_ref.shape[:2]
    ri = lax.broadcasted_iota(jnp.int32, (rows, 1), 0)
    zpad = jnp.zeros((LANES - rows, RET_DK), F32)
    for h in range(RET_HEADS):
        lg = lg_ref[h]
        sl = slice(h * RET_DK, (h + 1) * RET_DK)
        decay = []
        for j in range(dec_seq):
            dist = (ri - j).astype(F32)
            decay.append(jnp.where(dist >= 0, jnp.exp(jnp.maximum(dist, 0.0) * lg), 0.0))
        q_decay = jnp.exp((ri + 1).astype(F32) * lg)
        k_decay = jnp.where(ri < dec_seq, jnp.exp((dec_seq - 1 - ri).astype(F32) * lg), 0.0)
        carry = jnp.exp(jnp.full((1, 1), dec_seq, F32) * lg)
        for b in range(n_b):
            q = q_ref[b, :, sl]
            k = k_ref[b, :, sl]
            v = v_ref[b, :, sl]
            qb = q.astype(BF16).astype(F32)
            kb = k.astype(BF16).astype(F32)
            vb = v.astype(BF16).astype(F32)
            state = s_ref[b, h]
            inner = jnp.zeros((rows, RET_DV), F32)
            for j in range(dec_seq):
                score = jnp.sum(qb * kb[j:j + 1, :], axis=-1, keepdims=True) * decay[j]
                inner = inner + score.astype(BF16).astype(F32) * vb[j:j + 1, :]
            cross = jnp.dot(q.astype(BF16), state.astype(BF16), preferred_element_type=F32) * q_decay
            o_ref[b, :, sl] = _group_norm(inner + cross, g_ref[:, sl])
            kd_t = jnp.concatenate([k * k_decay, zpad], axis=0).T.astype(BF16)
            v_pad = jnp.concatenate([v, zpad], axis=0).astype(BF16)
            st_ref[b, h] = carry * state + jnp.dot(kd_t, v_pad, preferred_element_type=F32)


def _ret_sample(log_g, q, k, v, state, gn_g, dec_seq):
    dec_batch, rows = q.shape[:2]
    n_b = math.gcd(dec_batch, RET_SAMPLE_NB)
    tok = pl.BlockSpec((n_b, rows, D_MODEL), lambda b: (b, 0, 0))
    st = pl.BlockSpec((n_b, RET_HEADS, RET_DK, RET_DV), lambda b: (b, 0, 0, 0))
    return pl.pallas_call(
        functools.partial(_ret_sample_kernel, dec_seq=dec_seq),
        grid=(dec_batch // n_b,),
        in_specs=[pl.BlockSpec(memory_space=pltpu.SMEM), tok, tok, tok, st,
                  pl.BlockSpec((1, D_MODEL), lambda b: (0, 0))],
        out_specs=[tok, st],
        out_shape=[jax.ShapeDtypeStruct((dec_batch, rows, D_MODEL), F32),
                   jax.ShapeDtypeStruct(state.shape, F32)],
        compiler_params=_cparams(("parallel",)),
        name="retention_sample",
    )(log_g, q, k, v, state, gn_g)


def kernel(x_prompt, x_sample, cache_k, cache_v, state_ret, page_table, c_prompt, c_sample, ada_w, ada_b, norm_ffn1, norm_mix, norm_ffn2, ffn1_w_in, ffn1_w_out, ffn2_w_in, ffn2_w_out, w_in, w_out, lam_q1, lam_k1, lam_q2, lam_k2, subln_g, ret_norm_g, norm_final):
    batch, seq, _ = x_prompt.shape
    dec_batch, dec_seq, _ = x_sample.shape
    depth = ada_w.shape[0]
    page = cache_k.shape[2]
    past_len = page_table.shape[1] * page
    log_g = jnp.log1p(-jnp.exp2(-5.0 - jnp.arange(RET_HEADS, dtype=F32)))
    nf = norm_final[None, :]

    tab_p = _rotation_tables(seq, 0)
    sub = 8
    tab_s = [jnp.tile(t[:dec_seq], (dec_batch, 1)) for t in _rotation_tables(sub, past_len)]

    rows_s = dec_batch * dec_seq
    n_c = rows_s + batch
    c_all = jnp.concatenate([jnp.repeat(c_sample, dec_seq, axis=0), c_prompt,
                             jnp.zeros((-n_c % sub, D_MODEL), F32)], axis=0)

    yp = x_prompt.reshape(batch * seq, D_MODEL)
    ys = x_sample.reshape(dec_batch * dec_seq, D_MODEL)
    kp_l, vp_l, sp_l, ks_l, vs_l, ss_l = [], [], [], [], [], []
    for l in range(depth):
        lam_init = 0.8 - 0.6 * math.exp(-0.3 * l)
        mod_s = _adaln(c_all, ada_w[l], ada_b[l][None, :])
        mod_p = mod_s[rows_s:n_c].reshape(batch, 1, N_MOD * D_MODEL)
        w1i, w1o = ffn1_w_in[l].astype(BF16), ffn1_w_out[l].astype(BF16)
        w2i, w2o = ffn2_w_in[l].astype(BF16), ffn2_w_out[l].astype(BF16)
        wi, wo = w_in[l].astype(BF16), w_out[l].astype(BF16)
        g1, gm, g2 = norm_ffn1[l][None, :], norm_mix[l][None, :], norm_ffn2[l][None, :]
        lam_vecs = (lam_q1[l][None, :], lam_k1[l][None, :], lam_q2[l][None, :], lam_k2[l][None, :])
        sg = subln_g[l][None, :]
        gn = ret_norm_g[l][None, :]
        last = l == depth - 1

        if rows_s % FFN_TM == 0 and (batch * seq) % FFN_TM == 0:
            x1, x1_s = _ffn_two_groups(yp, mod_p, seq, ys, mod_s, (0, 1, 2), g1, w1i, w1o, nf)
        else:
            x1 = _ffn(yp, mod_p, "batch", seq, (0, 1, 2), g1, w1i, w1o, nf, False)
            x1_s = _ffn(ys, mod_s, "token", 1, (0, 1, 2), g1, w1i, w1o, nf, False)
        (q, ktf, ktb, vf, vb, qr, krt, vr, gr, ga, gb) = _proj(
            x1, mod_p, "batch", seq, gm, wi, tab_p, seq // min(PROJ_TM, seq), True, batch, seq)
        o_a = _attn_prompt(q, ktb, vb, lam_vecs, sg, batch, seq, lam_init)
        o_r, sp = _ret_prompt(log_g, qr, krt, vr, gn, batch, seq)
        yp = _ffn(x1, mod_p, "batch", seq, (6, 7, 8), g2, w2i, w2o, nf, last, mixer=(5, o_a, o_r, gr, ga, gb, wo))
        kp_l.append(jnp.transpose(ktf.reshape(batch, 2 * DA_HEADS, DA_DK, seq), (0, 3, 1, 2)))
        vp_l.append(vf.reshape(batch, seq, DA_HEADS, DA_DV))
        sp_l.append(sp)

        x1 = x1_s
        (q, k, v, qr, kr, vr, gr, ga, gb) = _proj(
            x1, mod_s, "token", 1, gm, wi, tab_s, rows_s // min(PROJ_TM, rows_s), False, dec_batch, dec_seq)
        qh = q.reshape(dec_batch, dec_seq, DA_HEADS, 2, DA_DK).transpose(0, 2, 3, 1, 4)
        eye = jnp.eye(2, dtype=F32)
        q_bd = (qh[:, :, :, :, None, :] * eye[None, None, :, None, :, None]).reshape(
            dec_batch, DA_HEADS, 2 * dec_seq, DA_DV)
        to_heads = lambda a: a.reshape(dec_batch, dec_seq, DA_HEADS, DA_DV).transpose(0, 2, 1, 3)
        cache_kt = jnp.transpose(cache_k[l], (0, 2, 3, 1)).reshape(-1, DA_HEADS, DA_DV, page)
        o_a = _attn_sample(page_table, cache_kt, cache_v[l], q_bd, to_heads(k), to_heads(v), lam_vecs, sg, lam_init)
        pad = lambda a: jnp.pad(a.reshape(dec_batch, dec_seq, D_MODEL), ((0, 0), (0, sub - dec_seq), (0, 0)))
        o_r, ssm = _ret_sample(log_g, pad(qr), pad(kr), pad(vr), state_ret[l], gn, dec_seq)
        o_r = o_r[:, :dec_seq].reshape(rows_s, D_MODEL)
        ys = _ffn(x1, mod_s, "token", 1, (6, 7, 8), g2, w2i, w2o, nf, last,
                  mixer=(5, o_a.reshape(rows_s, D_MODEL), o_r, gr, ga, gb, wo))
        ks_l.append(k.reshape(dec_batch, dec_seq, 2 * DA_HEADS, DA_DK))
        vs_l.append(v.reshape(dec_batch, dec_seq, DA_HEADS, DA_DV))
        ss_l.append(ssm)

    return (yp.reshape(batch, seq, D_MODEL), ys.reshape(dec_batch, dec_seq, D_MODEL),
            jnp.stack(kp_l), jnp.stack(vp_l), jnp.stack(sp_l), jnp.stack(ks_l), jnp.stack(vs_l), jnp.stack(ss_l))
```
